```python
import jax, jax.numpy as jnp
from jax import lax
import numpy as np

D_MODEL = 4096
BATCH = 1
SEQ = 8192
DEPTH = 4

CHUNK = 64
N_A = DEPTH // 2
N_B = DEPTH - N_A
A_WIDTH = D_MODEL
A_GROUPS = 8
A_BLOCK = 128
B_HEADS = 32
B_HEAD_DIM = D_MODEL // B_HEADS
Q_BLOCK = 128
D_FF = ((8 * D_MODEL // 3 + 255) // 256) * 256
CONV_W = 3
EPS = 1e-6

kernel_name = "yoco_gmlp_stickbreak_convffn"


def rms_norm(x, g):
    xf = x.astype(jnp.float32)
    y = xf * lax.rsqrt(jnp.mean(xf * xf, axis=-1, keepdims=True) + EPS)
    return (y * g.astype(jnp.float32)).astype(x.dtype)


def layer_norm(x, g, b):
    xf = x.astype(jnp.float32)
    mu = jnp.mean(xf, axis=-1, keepdims=True)
    var = jnp.mean(jnp.square(xf - mu), axis=-1, keepdims=True)
    y = (xf - mu) * lax.rsqrt(var + EPS)
    return (y * g.astype(jnp.float32) + b.astype(jnp.float32)).astype(x.dtype)


def causal_dwconv(h, w, b):
    K = w.shape[0]
    S = h.shape[1]
    hp = jnp.pad(h, ((0, 0), (K - 1, 0), (0, 0)))
    out = b
    for k in range(K):
        out = out + hp[:, k:k + S] * w[k]
    return out


def conv_ffn(h, w_up, conv_w, conv_b, w_down):
    z = causal_dwconv(h @ w_up, conv_w, conv_b)
    gate, val = jnp.split(z, 2, axis=-1)
    return (jax.nn.silu(gate) * val) @ w_down


def gmlp_mixer(h, w_in, ln_g, ln_b, w_s, b_s, w_out):
    Bsz, S, _ = h.shape
    z = jax.nn.gelu(h @ w_in, approximate=False)
    u, v = jnp.split(z, 2, axis=-1)
    v = layer_norm(v, ln_g, ln_b)
    v = v.reshape(Bsz, S // A_BLOCK, A_BLOCK, A_GROUPS, A_WIDTH // A_GROUPS)
    pos = jnp.arange(A_BLOCK)
    mask = (pos[None, :] // CHUNK) <= (pos[:, None] // CHUNK)
    ws = jnp.where(mask[None], w_s, jnp.zeros_like(w_s))
    sv = jnp.einsum('gij,bnjgc->bnigc', ws, v) + jnp.transpose(b_s)[None, None, :, :, None]
    y = u * sv.reshape(Bsz, S, A_WIDTH)
    return y @ w_out


def stick_breaking_attention(q, k, v):
    Bsz, S, H, Dh = q.shape
    scale = Dh ** -0.5
    outs = []
    for q0 in range(0, S, Q_BLOCK):
        L = q0 + Q_BLOCK
        qb = q[:, q0:L]
        kb = k[:, :L]
        vb = v[:, :L]
        z = jnp.einsum('bthd,bshd->bhts', qb, kb).astype(jnp.float32) * scale
        t_idx = q0 + jnp.arange(Q_BLOCK)
        s_idx = jnp.arange(L)
        causal = s_idx[None, :] < t_idx[:, None]
        log_one_minus = jnp.where(causal, jax.nn.log_sigmoid(-z), 0.0)
        later = lax.cumsum(log_one_minus, axis=3, reverse=True) - log_one_minus
        log_a = jax.nn.log_sigmoid(z) + later
        a = jnp.where(causal, jnp.exp(log_a), 0.0)
        outs.append(jnp.einsum('bhts,bshd->bthd', a.astype(vb.dtype), vb))
    return jnp.concatenate(outs, axis=1)


def setup_inputs(seed: int = 0) -> dict:
    key = jax.random.key(seed)
    ks = jax.random.split(key, 24)
    f32 = jnp.float32

    def nrm(k, shape, scale):
        return jax.random.normal(k, shape, f32) * scale

    def gain(k, shape):
        return 1.0 + 0.01 * jax.random.normal(k, shape, f32)

    return {
        "x": nrm(ks[0], (BATCH, SEQ, D_MODEL), 1.0),
        "a_norm_g": gain(ks[1], (N_A, D_MODEL)),
        "a_w_in": nrm(ks[2], (N_A, D_MODEL, 2 * A_WIDTH), D_MODEL ** -0.5),
        "a_ln_g": gain(ks[3], (N_A, A_WIDTH)),
        "a_ln_b": nrm(ks[4], (N_A, A_WIDTH), 0.01),
        "a_w_s": nrm(ks[5], (N_A, A_GROUPS, A_BLOCK, A_BLOCK), A_BLOCK ** -0.5),
        "a_b_s": gain(ks[6], (N_A, A_GROUPS, A_BLOCK)),
        "a_w_out": nrm(ks[7], (N_A, A_WIDTH, D_MODEL), A_WIDTH ** -0.5),
        "b_norm_g": gain(ks[8], (N_B, D_MODEL)),
        "b_w_q": nrm(ks[9], (N_B, D_MODEL, B_HEADS * B_HEAD_DIM), D_MODEL ** -0.5),
        "b_w_out": nrm(ks[10], (N_B, B_HEADS * B_HEAD_DIM, D_MODEL), (B_HEADS * B_HEAD_DIM) ** -0.5),
        "kv_norm_g": gain(ks[11], (D_MODEL,)),
        "w_kv": nrm(ks[12], (D_MODEL, 2 * B_HEADS * B_HEAD_DIM), D_MODEL ** -0.5),
        "ffn_norm_g": gain(ks[13], (DEPTH, D_MODEL)),
        "ffn_w_up": nrm(ks[14], (DEPTH, D_MODEL, 2 * D_FF), D_MODEL ** -0.5),
        "ffn_conv_w": nrm(ks[15], (DEPTH, CONV_W, 2 * D_FF), CONV_W ** -0.5),
        "ffn_conv_b": nrm(ks[16], (DEPTH, 2 * D_FF), 0.01),
        "ffn_w_down": nrm(ks[17], (DEPTH, D_FF, D_MODEL), D_FF ** -0.5),
        "final_norm_g": gain(ks[18], (D_MODEL,)),
    }


def reference(x, a_norm_g, a_w_in, a_ln_g, a_ln_b, a_w_s, a_b_s, a_w_out,
              b_norm_g, b_w_q, b_w_out, kv_norm_g, w_kv,
              ffn_norm_g, ffn_w_up, ffn_conv_w, ffn_conv_b, ffn_w_down, final_norm_g):
    Bsz, S, _ = x.shape
    k_shared = None
    v_shared = None
    for layer in range(DEPTH):
        if layer < N_A:
            i = layer
            x = x + gmlp_mixer(rms_norm(x, a_norm_g[i]), a_w_in[i], a_ln_g[i], a_ln_b[i],
                               a_w_s[i], a_b_s[i], a_w_out[i])
        else:
            i = layer - N_A
            q = (rms_norm(x, b_norm_g[i]) @ b_w_q[i]).reshape(Bsz, S, B_HEADS, B_HEAD_DIM)
            o = stick_breaking_attention(q, k_shared, v_shared)
            x = x + o.reshape(Bsz, S, B_HEADS * B_HEAD_DIM) @ b_w_out[i]
        x = x + conv_ffn(rms_norm(x, ffn_norm_g[layer]), ffn_w_up[layer],
                         ffn_conv_w[layer], ffn_conv_b[layer], ffn_w_down[layer])
        if layer == N_A - 1:
            kv = rms_norm(x, kv_norm_g) @ w_kv
            k_flat, v_flat = jnp.split(kv, 2, axis=-1)
            k_shared = k_flat.reshape(Bsz, S, B_HEADS, B_HEAD_DIM)
            v_shared = v_flat.reshape(Bsz, S, B_HEADS, B_HEAD_DIM)
    return rms_norm(x, final_norm_g)
```

```python
import functools

import jax
import jax.numpy as jnp
from jax import lax
from jax.experimental import pallas as pl
from jax.experimental.pallas import tpu as pltpu

EPS = 1e-6
CHUNK = 64
HEAD_DIM = 128
LANES = 128
SUBLANES = 8
MXU_COLS = 256
VMEM_LIMIT_BYTES = 56 * 1024 * 1024
ROW_TILE = 1024
KEY_BLOCK = 128
LOG_F32_TINY = -104.0

F32 = jnp.float32
BF16 = jnp.bfloat16


def _pick_tile(n, target):
    for align in (MXU_COLS, LANES):
        best = None
        t = align
        while t <= min(n, target):
            if n % t == 0:
                best = t
            t += align
        if best is not None:
            return best
    raise ValueError(f"no lane-aligned tile for {n}")


def _params(*sem):
    return pltpu.CompilerParams(dimension_semantics=sem, vmem_limit_bytes=VMEM_LIMIT_BYTES)


def _fold_lanes(a):
    out = a[:, 0:LANES]
    for k in range(1, a.shape[1] // LANES):
        out = out + a[:, k * LANES:(k + 1) * LANES]
    return out


def _inv_rms(ssq_ref, d_model):
    return lax.rsqrt(jnp.sum(ssq_ref[...], axis=1, keepdims=True) / d_model + EPS)


def _prep_kernel(x_ref, g_ref, h_ref, ssq_ref):
    x = x_ref[...]
    h_ref[...] = (x * g_ref[...]).astype(BF16)
    ssq_ref[...] = _fold_lanes(x * x)


def _prep(x, g):
    s, d = x.shape
    tm = min(256, s)
    return pl.pallas_call(
        _prep_kernel,
        grid=(s // tm,),
        in_specs=[pl.BlockSpec((tm, d), lambda i: (i, 0)),
                  pl.BlockSpec((1, d), lambda i: (0, 0))],
        out_specs=[pl.BlockSpec((tm, d), lambda i: (i, 0)),
                   pl.BlockSpec((tm, LANES), lambda i: (i, 0))],
        out_shape=[jax.ShapeDtypeStruct((s, d), BF16),
                   jax.ShapeDtypeStruct((s, LANES), F32)],
        compiler_params=_params("parallel"),
        name="prep",
    )(x, g.reshape(1, d))


def _final_norm_kernel(x_ref, g_ref, o_ref):
    x = x_ref[...]
    inv = lax.rsqrt(jnp.mean(x * x, axis=1, keepdims=True) + EPS)
    o_ref[...] = x * inv * g_ref[...]


def _final_norm(x, g):
    s, d = x.shape
    tm = min(256, s)
    return pl.pallas_call(
        _final_norm_kernel,
        grid=(s // tm,),
        in_specs=[pl.BlockSpec((tm, d), lambda i: (i, 0)),
                  pl.BlockSpec((1, d), lambda i: (0, 0))],
        out_specs=pl.BlockSpec((tm, d), lambda i: (i, 0)),
        out_shape=jax.ShapeDtypeStruct((s, d), F32),
        compiler_params=_params("parallel"),
        name="final_norm",
    )(x, g.reshape(1, d))


def _mm_in_kernel(h_ref, ssq_ref, w_ref, z_ref, vsum_ref, vsq_ref, rs_ref, *, d_model, nj_half):
    j = pl.program_id(1)

    @pl.when(j == 0)
    def _():
        rs_ref[...] = _inv_rms(ssq_ref, d_model)

    acc = jnp.dot(h_ref[...], w_ref[...].astype(BF16), preferred_element_type=F32)
    t = acc * rs_ref[...]
    z = 0.5 * t * (1.0 + lax.erf(t * (2.0 ** -0.5)))
    z_ref[...] = z.astype(BF16)

    @pl.when(j == nj_half)
    def _():
        vsum_ref[...] = jnp.zeros_like(vsum_ref)
        vsq_ref[...] = jnp.zeros_like(vsq_ref)

    @pl.when(j >= nj_half)
    def _():
        vsum_ref[...] += _fold_lanes(z)
        vsq_ref[...] += _fold_lanes(z * z)


def _mm_in(h, ssq, w):
    s, d = h.shape
    n = w.shape[1]
    tm = min(ROW_TILE, s)
    tn = _pick_tile(n // 2, 512)
    nj = n // tn
    kern = functools.partial(_mm_in_kernel, d_model=d, nj_half=nj // 2)
    return pl.pallas_call(
        kern,
        grid=(s // tm, nj),
        in_specs=[pl.BlockSpec((tm, d), lambda i, j: (i, 0)),
                  pl.BlockSpec((tm, LANES), lambda i, j: (i, 0)),
                  pl.BlockSpec((d, tn), lambda i, j: (0, j))],
        out_specs=[pl.BlockSpec((tm, tn), lambda i, j: (i, j)),
                   pl.BlockSpec((tm, LANES), lambda i, j: (i, 0)),
                   pl.BlockSpec((tm, LANES), lambda i, j: (i, 0))],
        out_shape=[jax.ShapeDtypeStruct((s, n), BF16),
                   jax.ShapeDtypeStruct((s, LANES), F32),
                   jax.ShapeDtypeStruct((s, LANES), F32)],
        scratch_shapes=[pltpu.VMEM((tm, 1), F32)],
        compiler_params=_params("parallel", "arbitrary"),
        name="gmlp_in",
    )(h, ssq, w)


def _gate_kernel(u_ref, v_ref, vsum_ref, vsq_ref, lng_ref, lnb_ref, ws_ref, bst_ref, y_ref,
                 *, width, n_groups, blk):
    mu = jnp.sum(vsum_ref[...], axis=1, keepdims=True) / width
    var = jnp.sum(vsq_ref[...], axis=1, keepdims=True) / width - mu * mu
    rstd = lax.rsqrt(var + EPS)
    vn = ((v_ref[...].astype(F32) - mu) * rstd * lng_ref[...] + lnb_ref[...]).astype(BF16)

    ri = lax.broadcasted_iota(jnp.int32, (blk, blk), 0)
    ci = lax.broadcasted_iota(jnp.int32, (blk, blk), 1)
    visible = (ci // CHUNK) <= (ri // CHUNK)
    gw = width // n_groups
    rows = u_ref.shape[0]
    for g in range(n_groups):
        wsg = jnp.where(visible, ws_ref[g], 0.0).astype(BF16)
        bias = bst_ref[:, g:g + 1]
        for n in range(rows // blk):
            r0 = n * blk
            sv = jnp.dot(wsg, vn[r0:r0 + blk, g * gw:(g + 1) * gw],
                         preferred_element_type=F32) + bias
            u = u_ref[r0:r0 + blk, g * gw:(g + 1) * gw].astype(F32)
            y_ref[r0:r0 + blk, g * gw:(g + 1) * gw] = (u * sv).astype(BF16)


def _gate(z, vsum, vsq, ln_g, ln_b, w_s, b_s):
    s = z.shape[0]
    width = z.shape[1] // 2
    n_groups, blk, _ = w_s.shape
    tb = min(512, s)
    kern = functools.partial(_gate_kernel, width=width, n_groups=n_groups, blk=blk)
    return pl.pallas_call(
        kern,
        grid=(s // tb,),
        in_specs=[pl.BlockSpec((tb, width), lambda i: (i, 0)),
                  pl.BlockSpec((tb, width), lambda i: (i, 1)),
                  pl.BlockSpec((tb, LANES), lambda i: (i, 0)),
                  pl.BlockSpec((tb, LANES), lambda i: (i, 0)),
                  pl.BlockSpec((1, width), lambda i: (0, 0)),
                  pl.BlockSpec((1, width), lambda i: (0, 0)),
                  pl.BlockSpec((n_groups, blk, blk), lambda i: (0, 0, 0)),
                  pl.BlockSpec((blk, n_groups), lambda i: (0, 0))],
        out_specs=pl.BlockSpec((tb, width), lambda i: (i, 0)),
        out_shape=jax.ShapeDtypeStruct((s, width), BF16),
        compiler_params=_params("parallel"),
        name="gmlp_gate",
    )(z, z, vsum, vsq, ln_g.reshape(1, width), ln_b.reshape(1, width), w_s, jnp.transpose(b_s))


def _mm_res_kernel(*refs, n_gains):
    a_ref, w_ref, xr_ref = refs[:3]
    g_refs = refs[3:3 + n_gains]
    x_out = refs[3 + n_gains]
    hn_outs = refs[4 + n_gains:4 + 2 * n_gains]
    ssq_out = refs[4 + 2 * n_gains]
    j = pl.program_id(1)

    acc = jnp.dot(a_ref[...], w_ref[...].astype(BF16), preferred_element_type=F32)
    xn = xr_ref[...] + acc
    x_out[...] = xn
    for g_ref, hn in zip(g_refs, hn_outs):
        hn[...] = (xn * g_ref[...]).astype(BF16)
    part = _fold_lanes(xn * xn)

    @pl.when(j == 0)
    def _():
        ssq_out[...] = part

    @pl.when(j > 0)
    def _():
        ssq_out[...] += part


def _mm_res(a, w, x_res, gains, tn_target):
    s, k = a.shape
    n = w.shape[1]
    tm = min(ROW_TILE, s)
    tn = _pick_tile(n, tn_target)
    n_gains = len(gains)
    a_mode = {} if 2 * tm * k * 2 <= 16 * 1024 * 1024 else {"pipeline_mode": pl.Buffered(1)}
    kern = functools.partial(_mm_res_kernel, n_gains=n_gains)
    outs = pl.pallas_call(
        kern,
        grid=(s // tm, n // tn),
        in_specs=[pl.BlockSpec((tm, k), lambda i, j: (i, 0), **a_mode),
                  pl.BlockSpec((k, tn), lambda i, j: (0, j)),
                  pl.BlockSpec((tm, tn), lambda i, j: (i, j))]
                 + [pl.BlockSpec((1, tn), lambda i, j: (0, j))] * n_gains,
        out_specs=[pl.BlockSpec((tm, tn), lambda i, j: (i, j))] * (1 + n_gains)
                  + [pl.BlockSpec((tm, LANES), lambda i, j: (i, 0))],
        out_shape=[jax.ShapeDtypeStruct((s, n), F32)]
                  + [jax.ShapeDtypeStruct((s, n), BF16)] * n_gains
                  + [jax.ShapeDtypeStruct((s, LANES), F32)],
        compiler_params=_params("parallel", "arbitrary"),
        name="mm_res",
    )(a, w, x_res, *[g.reshape(1, n) for g in gains])
    return outs[0], list(outs[1:1 + n_gains]), outs[1 + n_gains]


def _shift_rows(z, prev, shift):
    row = lax.broadcasted_iota(jnp.int32, prev.shape, 0)
    top = jnp.where(row < shift, pltpu.roll(prev, shift, 0), pltpu.roll(z[0:SUBLANES], shift, 0))
    return jnp.concatenate([top, pltpu.roll(z, shift, 0)[SUBLANES:]], axis=0)


def _ffn_up_kernel(h_ref, ssq_ref, wg_ref, wv_ref, cwg_ref, cwv_ref, cbg_ref, cbv_ref,
                   a_ref, rs_ref, tail_ref, *, d_model, nj):
    i = pl.program_id(0)
    j = pl.program_id(1)

    @pl.when(j == 0)
    def _():
        rs_ref[...] = _inv_rms(ssq_ref, d_model)

    h = h_ref[...]
    rs = rs_ref[...]
    tm = h.shape[0]

    def conv_half(w_ref, cw_ref, cb_ref, slot):
        z = jnp.dot(h, w_ref[...].astype(BF16), preferred_element_type=F32) * rs

        @pl.when(i == 0)
        def _():
            tail_ref[slot] = jnp.zeros(tail_ref.shape[1:], F32)

        prev = tail_ref[slot]
        tail_ref[slot] = z[tm - SUBLANES:, :]
        cw = cw_ref[...]
        return (cw[0:1] * _shift_rows(z, prev, 2) + cw[1:2] * _shift_rows(z, prev, 1)
                + cw[2:3] * z + cb_ref[...])

    gate = conv_half(wg_ref, cwg_ref, cbg_ref, j)
    val = conv_half(wv_ref, cwv_ref, cbv_ref, nj + j)
    a_ref[...] = (gate * jax.nn.sigmoid(gate) * val).astype(BF16)


def _ffn_up(h, ssq, w_up, conv_w, conv_b):
    s, d = h.shape
    d_ff = w_up.shape[1] // 2
    assert conv_w.shape[0] == 3
    tm = min(ROW_TILE, s)
    tn = _pick_tile(d_ff, 256)
    nj = d_ff // tn
    kern = functools.partial(_ffn_up_kernel, d_model=d, nj=nj)
    cb = conv_b.reshape(1, 2 * d_ff)
    return pl.pallas_call(
        kern,
        grid=(s // tm, nj),
        in_specs=[pl.BlockSpec((tm, d), lambda i, j: (i, 0)),
                  pl.BlockSpec((tm, LANES), lambda i, j: (i, 0)),
                  pl.BlockSpec((d, tn), lambda i, j: (0, j)),
                  pl.BlockSpec((d, tn), lambda i, j: (0, j + nj)),
                  pl.BlockSpec((3, tn), lambda i, j: (0, j)),
                  pl.BlockSpec((3, tn), lambda i, j: (0, j + nj)),
                  pl.BlockSpec((1, tn), lambda i, j: (0, j)),
                  pl.BlockSpec((1, tn), lambda i, j: (0, j + nj))],
        out_specs=pl.BlockSpec((tm, tn), lambda i, j: (i, j)),
        out_shape=jax.ShapeDtypeStruct((s, d_ff), BF16),
        scratch_shapes=[pltpu.VMEM((tm, 1), F32),
                        pltpu.VMEM((2 * nj, SUBLANES, tn), F32)],
        compiler_params=_params("arbitrary", "arbitrary"),
        name="ffn_up",
    )(h, ssq, w_up, w_up, conv_w, conv_w, cb, cb)


def _mm_heads_kernel(h_ref, ssq_ref, w_ref, o_ref, rs_ref, *, d_model):
    @pl.when(pl.program_id(1) == 0)
    def _():
        rs_ref[...] = _inv_rms(ssq_ref, d_model)

    acc = jnp.dot(h_ref[...], w_ref[...].astype(BF16), preferred_element_type=F32) * rs_ref[...]
    for hh in range(o_ref.shape[0]):
        o_ref[hh] = acc[:, hh * HEAD_DIM:(hh + 1) * HEAD_DIM].astype(BF16)


def _mm_heads(h, ssq, w):
    s, d = h.shape
    n = w.shape[1]
    tm = min(ROW_TILE, s)
    tn = _pick_tile(n, 512)
    hpt = tn // HEAD_DIM
    kern = functools.partial(_mm_heads_kernel, d_model=d)
    return pl.pallas_call(
        kern,
        grid=(s // tm, n // tn),
        in_specs=[pl.BlockSpec((tm, d), lambda i, j: (i, 0)),
                  pl.BlockSpec((tm, LANES), lambda i, j: (i, 0)),
                  pl.BlockSpec((d, tn), lambda i, j: (0, j))],
        out_specs=pl.BlockSpec((hpt, tm, HEAD_DIM), lambda i, j: (j, i, 0)),
        out_shape=jax.ShapeDtypeStruct((n // HEAD_DIM, s, HEAD_DIM), BF16),
        scratch_shapes=[pltpu.VMEM((tm, 1), F32)],
        compiler_params=_params("parallel", "arbitrary"),
        name="mm_heads",
    )(h, ssq, w)


def _stick_kernel(q_ref, k_ref, v_ref, o_ref, *, scale):
    tq = KEY_BLOCK
    bk = KEY_BLOCK
    n_qblocks = q_ref.shape[1] // tq

    kj = lax.broadcasted_iota(jnp.int32, (bk, 2 * bk), 0)
    ks = lax.broadcasted_iota(jnp.int32, (bk, 2 * bk), 1)
    suffix = jnp.where((kj > ks) | (ks >= bk), 1.0, 0.0).astype(BF16)
    r_io = lax.broadcasted_iota(jnp.int32, (tq, bk), 0)
    c_io = lax.broadcasted_iota(jnp.int32, (tq, bk), 1)
    causal = c_io < r_io

    def sweep_block(q, kb, decay, acc, masked):
        k = k_ref[0, pl.ds(pl.multiple_of(kb * bk, bk), bk), :]
        v = v_ref[0, pl.ds(pl.multiple_of(kb * bk, bk), bk), :]
        z = lax.dot_general(q, k, (((1,), (1,)), ((), ())), preferred_element_type=F32) * scale
        sp = jnp.maximum(z, 0.0) + jnp.log1p(jnp.exp(-jnp.abs(z)))
        if masked:
            sp = jnp.where(causal, sp, 0.0)
        hi = sp.astype(BF16)
        lo = (sp - hi.astype(F32)).astype(BF16)
        sums = (jnp.dot(hi, suffix, preferred_element_type=F32)
                + jnp.dot(lo, suffix, preferred_element_type=F32))
        log_a = z - sp - (decay + sums[:, :bk])
        a = jnp.exp(log_a)
        if masked:
            a = jnp.where(causal, a, 0.0)
        acc = acc + jnp.dot(a.astype(BF16), v, preferred_element_type=F32)
        return decay + sums[:, bk:], acc

    def q_block(qi, carry):
        q = q_ref[0, pl.ds(pl.multiple_of(qi * tq, tq), tq), :]
        decay, acc = sweep_block(q, qi, jnp.zeros((tq, bk), F32), jnp.zeros((tq, HEAD_DIM), F32), True)

        def cond(c):
            kb, live, _, _ = c
            return (kb >= 0) & live

        def body(c):
            kb, _, decay, acc = c
            decay, acc = sweep_block(q, kb, decay, acc, False)
            live = jnp.min(decay) < -LOG_F32_TINY
            return kb - 1, live, decay, acc

        _, _, _, acc = lax.while_loop(cond, body, (qi - 1, jnp.min(decay) < -LOG_F32_TINY, decay, acc))
        o_ref[pl.ds(pl.multiple_of(qi * tq, tq), tq), :] = acc.astype(BF16)
        return carry

    lax.fori_loop(0, n_qblocks, q_block, 0)


def _stick_attention(qh, kvh):
    n_heads, s, _ = qh.shape
    kern = functools.partial(_stick_kernel, scale=HEAD_DIM ** -0.5)
    return pl.pallas_call(
        kern,
        grid=(n_heads,),
        in_specs=[pl.BlockSpec((1, s, HEAD_DIM), lambda h: (h, 0, 0)),
                  pl.BlockSpec((1, s, HEAD_DIM), lambda h: (h, 0, 0)),
                  pl.BlockSpec((1, s, HEAD_DIM), lambda h: (h + n_heads, 0, 0))],
        out_specs=pl.BlockSpec((s, HEAD_DIM), lambda h: (0, h)),
        out_shape=jax.ShapeDtypeStruct((s, n_heads * HEAD_DIM), BF16),
        compiler_params=_params("parallel"),
        name="stick_attention",
    )(qh, kvh, kvh)


def kernel(x, a_norm_g, a_w_in, a_ln_g, a_ln_b, a_w_s, a_b_s, a_w_out, b_norm_g, b_w_q, b_w_out,
           kv_norm_g, w_kv, ffn_norm_g, ffn_w_up, ffn_conv_w, ffn_conv_b, ffn_w_down, final_norm_g):
    bsz, s, d = x.shape
    assert bsz == 1
    n_a = a_w_in.shape[0]
    n_b = b_w_q.shape[0]
    depth = n_a + n_b
    xs = x[0]

    h, ssq = _prep(xs, a_norm_g[0])
    hkv = None
    kvh = None
    for layer in range(depth):
        if layer < n_a:
            i = layer
            z, vsum, vsq = _mm_in(h, ssq, a_w_in[i])
            y = _gate(z, vsum, vsq, a_ln_g[i], a_ln_b[i], a_w_s[i], a_b_s[i])
            xs, (h,), ssq = _mm_res(y, a_w_out[i], xs, [ffn_norm_g[layer]], 512)
        else:
            i = layer - n_a
            qh = _mm_heads(h, ssq, b_w_q[i])
            o = _stick_attention(qh, kvh)
            xs, (h,), ssq = _mm_res(o, b_w_out[i], xs, [ffn_norm_g[layer]], 512)
        a = _ffn_up(h, ssq, ffn_w_up[layer], ffn_conv_w[layer], ffn_conv_b[layer])
        if layer == n_a - 1:
            xs, (hkv, h), ssq = _mm_res(a, ffn_w_down[layer], xs, [kv_norm_g, b_norm_g[0]], 256)
            kvh = _mm_heads(hkv, ssq, w_kv)
        elif layer == depth - 1:
            xs, _, ssq = _mm_res(a, ffn_w_down[layer], xs, [], 256)
        else:
            nxt = a_norm_g[layer + 1] if layer + 1 < n_a else b_norm_g[layer + 1 - n_a]
            xs, (h,), ssq = _mm_res(a, ffn_w_down[layer], xs, [nxt], 256)
    return _final_norm(xs, final_norm_g)[None]
```

```python
import functools

import jax
import jax.numpy as jnp
from jax import lax
from jax.experimental import pallas as pl
from jax.experimental.pallas import tpu as pltpu

EPS = 1e-6
CHUNK = 64
HEAD_DIM = 128
LANES = 128
SUBLANES = 8
MXU_COLS = 256
VMEM_LIMIT_BYTES = 56 * 1024 * 1024
ROW_TILE = 1024
FFN_EPILOGUE_ROWS = 256
KEY_BLOCK = 128
STICK_CHAINS = 8
LOG_F32_TINY = -104.0

F32 = jnp.float32
BF16 = jnp.bfloat16


def _pick_tile(n, target):
    for align in (MXU_COLS, LANES):
        best = None
        t = align
        while t <= min(n, target):
            if n % t == 0:
                best = t
            t += align
        if best is not None:
            return best
    raise ValueError(f"no lane-aligned tile for {n}")


def _params(*sem):
    return pltpu.CompilerParams(dimension_semantics=sem, vmem_limit_bytes=VMEM_LIMIT_BYTES)


def _bf16(w):
    return w if w.dtype == BF16 else w.astype(BF16)


def _fold_lanes(a):
    out = a[:, 0:LANES]
    for k in range(1, a.shape[1] // LANES):
        out = out + a[:, k * LANES:(k + 1) * LANES]
    return out


def _inv_rms(ssq_ref, d_model):
    return lax.rsqrt(jnp.sum(ssq_ref[...], axis=1, keepdims=True) / d_model + EPS)


def _prep_kernel(x_ref, g_ref, h_ref, ssq_ref):
    x = x_ref[...]
    h_ref[...] = (x * g_ref[...]).astype(BF16)
    ssq_ref[...] = _fold_lanes(x * x)


def _prep(x, g):
    s, d = x.shape
    tm = min(256, s)
    return pl.pallas_call(
        _prep_kernel,
        grid=(s // tm,),
        in_specs=[pl.BlockSpec((tm, d), lambda i: (i, 0)),
                  pl.BlockSpec((1, d), lambda i: (0, 0))],
        out_specs=[pl.BlockSpec((tm, d), lambda i: (i, 0)),
                   pl.BlockSpec((tm, LANES), lambda i: (i, 0))],
        out_shape=[jax.ShapeDtypeStruct((s, d), BF16),
                   jax.ShapeDtypeStruct((s, LANES), F32)],
        compiler_params=_params("parallel"),
        name="prep",
    )(x, g.reshape(1, d))


def _final_norm_kernel(x_ref, g_ref, o_ref):
    x = x_ref[...]
    inv = lax.rsqrt(jnp.mean(x * x, axis=1, keepdims=True) + EPS)
    o_ref[...] = x * inv * g_ref[...]


def _final_norm(x, g):
    s, d = x.shape
    tm = min(256, s)
    return pl.pallas_call(
        _final_norm_kernel,
        grid=(s // tm,),
        in_specs=[pl.BlockSpec((tm, d), lambda i: (i, 0)),
                  pl.BlockSpec((1, d), lambda i: (0, 0))],
        out_specs=pl.BlockSpec((tm, d), lambda i: (i, 0)),
        out_shape=jax.ShapeDtypeStruct((s, d), F32),
        compiler_params=_params("parallel"),
        name="final_norm",
    )(x, g.reshape(1, d))


def _mm_in_kernel(h_ref, ssq_ref, w_ref, z_ref, vsum_ref, vsq_ref, rs_ref, *, d_model, nj_half):
    j = pl.program_id(1)

    @pl.when(j == 0)
    def _():
        rs_ref[...] = _inv_rms(ssq_ref, d_model)

    acc = jnp.dot(h_ref[...], _bf16(w_ref[...]), preferred_element_type=F32)
    t = acc * rs_ref[...]
    z = 0.5 * t * (1.0 + lax.erf(t * (2.0 ** -0.5)))
    z_ref[...] = z.astype(BF16)

    @pl.when(j == nj_half)
    def _():
        vsum_ref[...] = jnp.zeros_like(vsum_ref)
        vsq_ref[...] = jnp.zeros_like(vsq_ref)

    @pl.when(j >= nj_half)
    def _():
        vsum_ref[...] += _fold_lanes(z)
        vsq_ref[...] += _fold_lanes(z * z)


def _mm_in(h, ssq, w_stack, layer):
    s, d = h.shape
    n = w_stack.shape[2]
    tm = min(ROW_TILE, s)
    tn = _pick_tile(n // 2, 512)
    nj = n // tn
    kern = functools.partial(_mm_in_kernel, d_model=d, nj_half=nj // 2)
    return pl.pallas_call(
        kern,
        grid=(s // tm, nj),
        in_specs=[pl.BlockSpec((tm, d), lambda i, j: (i, 0)),
                  pl.BlockSpec((tm, LANES), lambda i, j: (i, 0)),
                  pl.BlockSpec((None, d, tn), lambda i, j: (layer, 0, j))],
        out_specs=[pl.BlockSpec((tm, tn), lambda i, j: (i, j)),
                   pl.BlockSpec((tm, LANES), lambda i, j: (i, 0)),
                   pl.BlockSpec((tm, LANES), lambda i, j: (i, 0))],
        out_shape=[jax.ShapeDtypeStruct((s, n), BF16),
                   jax.ShapeDtypeStruct((s, LANES), F32),
                   jax.ShapeDtypeStruct((s, LANES), F32)],
        scratch_shapes=[pltpu.VMEM((tm, 1), F32)],
        compiler_params=_params("parallel", "arbitrary"),
        name="gmlp_in",
    )(h, ssq, w_stack)


def _gate_kernel(u_ref, v_ref, vsum_ref, vsq_ref, lng_ref, lnb_ref, ws_ref, bst_ref, y_ref,
                 *, width, n_groups, blk):
    mu = jnp.sum(vsum_ref[...], axis=1, keepdims=True) / width
    var = jnp.sum(vsq_ref[...], axis=1, keepdims=True) / width - mu * mu
    rstd = lax.rsqrt(var + EPS)
    vn = ((v_ref[...].astype(F32) - mu) * rstd * lng_ref[...] + lnb_ref[...]).astype(BF16)

    ri = lax.broadcasted_iota(jnp.int32, (blk, blk), 0)
    ci = lax.broadcasted_iota(jnp.int32, (blk, blk), 1)
    visible = (ci // CHUNK) <= (ri // CHUNK)
    gw = width // n_groups
    rows = u_ref.shape[0]
    for g in range(n_groups):
        wsg = jnp.where(visible, ws_ref[g], 0.0).astype(BF16)
        bias = bst_ref[:, g:g + 1]
        for n in range(rows // blk):
            r0 = n * blk
            sv = jnp.dot(wsg, vn[r0:r0 + blk, g * gw:(g + 1) * gw],
                         preferred_element_type=F32) + bias
            u = u_ref[r0:r0 + blk, g * gw:(g + 1) * gw].astype(F32)
            y_ref[r0:r0 + blk, g * gw:(g + 1) * gw] = (u * sv).astype(BF16)


def _gate(z, vsum, vsq, ln_g, ln_b, w_s, b_s):
    s = z.shape[0]
    width = z.shape[1] // 2
    n_groups, blk, _ = w_s.shape
    tb = min(512, s)
    kern = functools.partial(_gate_kernel, width=width, n_groups=n_groups, blk=blk)
    return pl.pallas_call(
        kern,
        grid=(s // tb,),
        in_specs=[pl.BlockSpec((tb, width), lambda i: (i, 0)),
                  pl.BlockSpec((tb, width), lambda i: (i, 1)),
                  pl.BlockSpec((tb, LANES), lambda i: (i, 0)),
                  pl.BlockSpec((tb, LANES), lambda i: (i, 0)),
                  pl.BlockSpec((1, width), lambda i: (0, 0)),
                  pl.BlockSpec((1, width), lambda i: (0, 0)),
                  pl.BlockSpec((n_groups, blk, blk), lambda i: (0, 0, 0)),
                  pl.BlockSpec((blk, n_groups), lambda i: (0, 0))],
        out_specs=pl.BlockSpec((tb, width), lambda i: (i, 0)),
        out_shape=jax.ShapeDtypeStruct((s, width), BF16),
        compiler_params=_params("parallel"),
        name="gmlp_gate",
    )(z, z, vsum, vsq, ln_g.reshape(1, width), ln_b.reshape(1, width), w_s, jnp.transpose(b_s))


def _mm_res_kernel(*refs, n_gains):
    a_ref, w_ref, xr_ref = refs[:3]
    g_refs = refs[3:3 + n_gains]
    x_out = refs[3 + n_gains]
    hn_outs = refs[4 + n_gains:4 + 2 * n_gains]
    ssq_out = refs[4 + 2 * n_gains]
    j = pl.program_id(1)

    acc = jnp.dot(a_ref[...], _bf16(w_ref[...]), preferred_element_type=F32)
    xn = xr_ref[...] + acc
    x_out[...] = xn
    for g_ref, hn in zip(g_refs, hn_outs):
        hn[...] = (xn * g_ref[...]).astype(BF16)
    part = _fold_lanes(xn * xn)

    @pl.when(j == 0)
    def _():
        ssq_out[...] = part

    @pl.when(j > 0)
    def _():
        ssq_out[...] += part


def _mm_res(a, w_stack, layer, x_res, gains, tn_target):
    s, k = a.shape
    n = w_stack.shape[2]
    tm = min(ROW_TILE, s)
    tn = _pick_tile(n, tn_target)
    n_gains = len(gains)
    a_mode = {} if 2 * tm * k * 2 <= 16 * 1024 * 1024 else {"pipeline_mode": pl.Buffered(1)}
    kern = functools.partial(_mm_res_kernel, n_gains=n_gains)
    outs = pl.pallas_call(
        kern,
        grid=(s // tm, n // tn),
        in_specs=[pl.BlockSpec((tm, k), lambda i, j: (i, 0), **a_mode),
                  pl.BlockSpec((None, k, tn), lambda i, j: (layer, 0, j)),
                  pl.BlockSpec((tm, tn), lambda i, j: (i, j))]
                 + [pl.BlockSpec((1, tn), lambda i, j: (0, j))] * n_gains,
        out_specs=[pl.BlockSpec((tm, tn), lambda i, j: (i, j))] * (1 + n_gains)
                  + [pl.BlockSpec((tm, LANES), lambda i, j: (i, 0))],
        out_shape=[jax.ShapeDtypeStruct((s, n), F32)]
                  + [jax.ShapeDtypeStruct((s, n), BF16)] * n_gains
                  + [jax.ShapeDtypeStruct((s, LANES), F32)],
        compiler_params=_params("parallel", "arbitrary"),
        name="mm_res",
    )(a, w_stack, x_res, *[g.reshape(1, n) for g in gains])
    return outs[0], list(outs[1:1 + n_gains]), outs[1 + n_gains]


def _ffn_up_kernel(h_ref, ssq_ref, wg_ref, wv_ref, cwg_ref, cwv_ref, cbg_ref, cbv_ref,
                   a_ref, z_ref, rs_ref, tail_ref, *, d_model):
    i = pl.program_id(0)
    j = pl.program_id(1)
    tm, tn = a_ref.shape
    halo = SUBLANES

    @pl.when(j == 0)
    def _():
        rs_ref[...] = _inv_rms(ssq_ref, d_model)

    @pl.when((i == 0) & (j == 0))
    def _():
        tail_ref[...] = jnp.zeros_like(tail_ref)

    w = jnp.concatenate([_bf16(wg_ref[...]), _bf16(wv_ref[...])], axis=1)
    z_ref[halo:, :] = jnp.dot(h_ref[...], w, preferred_element_type=F32) * rs_ref[...]
    z_ref[0:halo, :] = jnp.where(i > 0, tail_ref[j], 0.0)
    tail_ref[j] = z_ref[tm:tm + halo, :]

    cw = jnp.concatenate([cwg_ref[...], cwv_ref[...]], axis=1)
    cb = jnp.concatenate([cbg_ref[...], cbv_ref[...]], axis=1)
    rows = min(FFN_EPILOGUE_ROWS, tm)
    for r0 in range(0, tm, rows):
        conv = (cw[0:1] * z_ref[halo - 2 + r0:halo - 2 + r0 + rows, :]
                + cw[1:2] * z_ref[halo - 1 + r0:halo - 1 + r0 + rows, :]
                + cw[2:3] * z_ref[halo + r0:halo + r0 + rows, :] + cb)
        gate = conv[:, :tn]
        a_ref[r0:r0 + rows, :] = (gate * jax.nn.sigmoid(gate) * conv[:, tn:]).astype(BF16)


def _ffn_up(h, ssq, w_up_stack, layer, conv_w, conv_b):
    s, d = h.shape
    d_ff = w_up_stack.shape[2] // 2
    assert conv_w.shape[0] == 3
    tm = min(ROW_TILE, s)
    tn = _pick_tile(d_ff, 256)
    nj = d_ff // tn
    kern = functools.partial(_ffn_up_kernel, d_model=d)
    cb = conv_b.reshape(1, 2 * d_ff)
    return pl.pallas_call(
        kern,
        grid=(s // tm, nj),
        in_specs=[pl.BlockSpec((tm, d), lambda i, j: (i, 0)),
                  pl.BlockSpec((tm, LANES), lambda i, j: (i, 0)),
                  pl.BlockSpec((None, d, tn), lambda i, j: (layer, 0, j)),
                  pl.BlockSpec((None, d, tn), lambda i, j: (layer, 0, j + nj)),
                  pl.BlockSpec((3, tn), lambda i, j: (0, j)),
                  pl.BlockSpec((3, tn), lambda i, j: (0, j + nj)),
                  pl.BlockSpec((1, tn), lambda i, j: (0, j)),
                  pl.BlockSpec((1, tn), lambda i, j: (0, j + nj))],
        out_specs=pl.BlockSpec((tm, tn), lambda i, j: (i, j)),
        out_shape=jax.ShapeDtypeStruct((s, d_ff), BF16),
        scratch_shapes=[pltpu.VMEM((SUBLANES + tm, 2 * tn), F32),
                        pltpu.VMEM((tm, 1), F32),
                        pltpu.VMEM((nj, SUBLANES, 2 * tn), F32)],
        compiler_params=_params("arbitrary", "arbitrary"),
        name="ffn_up",
    )(h, ssq, w_up_stack, w_up_stack, conv_w, conv_w, cb, cb)


def _mm_heads_kernel(h_ref, ssq_ref, w_ref, o_ref, rs_ref, *, d_model):
    @pl.when(pl.program_id(1) == 0)
    def _():
        rs_ref[...] = _inv_rms(ssq_ref, d_model)

    acc = jnp.dot(h_ref[...], _bf16(w_ref[...]), preferred_element_type=F32) * rs_ref[...]
    for hh in range(o_ref.shape[0]):
        o_ref[hh] = acc[:, hh * HEAD_DIM:(hh + 1) * HEAD_DIM].astype(BF16)


def _mm_heads(h, ssq, w_stack, layer):
    s, d = h.shape
    n = w_stack.shape[2]
    tm = min(ROW_TILE, s)
    tn = _pick_tile(n, 512)
    hpt = tn // HEAD_DIM
    kern = functools.partial(_mm_heads_kernel, d_model=d)
    return pl.pallas_call(
        kern,
        grid=(s // tm, n // tn),
        in_specs=[pl.BlockSpec((tm, d), lambda i, j: (i, 0)),
                  pl.BlockSpec((tm, LANES), lambda i, j: (i, 0)),
                  pl.BlockSpec((None, d, tn), lambda i, j: (layer, 0, j))],
        out_specs=pl.BlockSpec((hpt, tm, HEAD_DIM), lambda i, j: (j, i, 0)),
        out_shape=jax.ShapeDtypeStruct((n // HEAD_DIM, s, HEAD_DIM), BF16),
        scratch_shapes=[pltpu.VMEM((tm, 1), F32)],
        compiler_params=_params("parallel", "arbitrary"),
        name="mm_heads",
    )(h, ssq, w_stack)


def _stick_kernel(q_ref, k_ref, v_ref, o_ref, decay_ref, acc_ref, *, scale, n_chains):
    tq = KEY_BLOCK
    bk = KEY_BLOCK
    n_groups = q_ref.shape[1] // (tq * n_chains)

    kj = lax.broadcasted_iota(jnp.int32, (2 * bk, 2 * bk), 0) % bk
    ks = lax.broadcasted_iota(jnp.int32, (2 * bk, 2 * bk), 1)
    suffix = jnp.where((kj > ks) | (ks >= bk), 1.0, 0.0).astype(BF16)
    r_io = lax.broadcasted_iota(jnp.int32, (tq, bk), 0)
    c_io = lax.broadcasted_iota(jnp.int32, (tq, bk), 1)
    causal = c_io < r_io

    def sweep(qi0, dist, diagonal):
        chains = range(n_chains)
        kbs = [qi0 + c - dist for c in chains]
        keeps = [causal if diagonal else (kb >= 0) for kb in kbs]
        k0s = [pl.multiple_of(jnp.maximum(kb, 0) * bk, bk) for kb in kbs]
        zs = []
        for c in chains:
            q = q_ref[0, pl.ds(pl.multiple_of((qi0 + c) * tq, tq), tq), :]
            k = k_ref[0, pl.ds(k0s[c], bk), :]
            zs.append(lax.dot_general(q, k, (((1,), (1,)), ((), ())),
                                      preferred_element_type=F32) * scale)
        sps = [jnp.where(keeps[c], jnp.maximum(zs[c], 0.0) + jnp.log1p(jnp.exp(-jnp.abs(zs[c]))), 0.0)
               for c in chains]
        sums = []
        for c in chains:
            hi = sps[c].astype(BF16)
            lo = (sps[c] - hi.astype(F32)).astype(BF16)
            sums.append(jnp.dot(jnp.concatenate([hi, lo], axis=1), suffix,
                                preferred_element_type=F32))
        weights = [jnp.where(keeps[c], jnp.exp(zs[c] - sps[c] - (decay_ref[c] + sums[c][:, :bk])), 0.0)
                   for c in chains]
        decays = []
        for c in chains:
            v = v_ref[0, pl.ds(k0s[c], bk), :]
            acc_ref[c] += jnp.dot(weights[c].astype(BF16), v, preferred_element_type=F32)
            decays.append(decay_ref[c] + sums[c][:, bk:])
            decay_ref[c] = decays[c]
        return decays

    def any_live(decays, qi0, next_dist):
        m = None
        for c, dec in enumerate(decays):
            dc = jnp.where(qi0 + c - next_dist >= 0, dec, -2.0 * LOG_F32_TINY)
            m = dc if m is None else jnp.minimum(m, dc)
        return jnp.min(m) < -LOG_F32_TINY

    def group(g, carry):
        qi0 = g * n_chains
        decay_ref[...] = jnp.zeros_like(decay_ref)
        acc_ref[...] = jnp.zeros_like(acc_ref)
        decays = sweep(qi0, 0, True)

        def cond(state):
            return state[1]

        def body(state):
            dist = state[0]
            decays = sweep(qi0, dist, False)
            return dist + 1, any_live(decays, qi0, dist + 1)

        lax.while_loop(cond, body, (jnp.int32(1), any_live(decays, qi0, 1)))
        for c in range(n_chains):
            o_ref[pl.ds(pl.multiple_of((qi0 + c) * tq, tq), tq), :] = acc_ref[c].astype(BF16)
        return carry

    lax.fori_loop(0, n_groups, group, 0)


def _stick_attention(qh, kvh):
    n_heads, s, _ = qh.shape
    n_chains = min(STICK_CHAINS, s // KEY_BLOCK)
    assert s % (KEY_BLOCK * n_chains) == 0
    kern = functools.partial(_stick_kernel, scale=HEAD_DIM ** -0.5, n_chains=n_chains)
    return pl.pallas_call(
        kern,
        grid=(n_heads,),
        in_specs=[pl.BlockSpec((1, s, HEAD_DIM), lambda h: (h, 0, 0)),
                  pl.BlockSpec((1, s, HEAD_DIM), lambda h: (h, 0, 0)),
                  pl.BlockSpec((1, s, HEAD_DIM), lambda h: (h + n_heads, 0, 0))],
        out_specs=pl.BlockSpec((s, HEAD_DIM), lambda h: (0, h)),
        out_shape=jax.ShapeDtypeStruct((s, n_heads * HEAD_DIM), BF16),
        scratch_shapes=[pltpu.VMEM((n_chains, KEY_BLOCK, KEY_BLOCK), F32),
                        pltpu.VMEM((n_chains, KEY_BLOCK, HEAD_DIM), F32)],
        compiler_params=_params("parallel"),
        name="stick_attention",
    )(qh, kvh, kvh)


def kernel(x, a_norm_g, a_w_in, a_ln_g, a_ln_b, a_w_s, a_b_s, a_w_out, b_norm_g, b_w_q, b_w_out,
           kv_norm_g, w_kv, ffn_norm_g, ffn_w_up, ffn_conv_w, ffn_conv_b, ffn_w_down, final_norm_g):
    bsz, s, d = x.shape
    assert bsz == 1
    n_a = a_w_in.shape[0]
    n_b = b_w_q.shape[0]
    depth = n_a + n_b
    xs = x[0]
    a_w_out = a_w_out.astype(BF16)
    b_w_out = b_w_out.astype(BF16)
    ffn_w_down = ffn_w_down.astype(BF16)

    h, ssq = _prep(xs, a_norm_g[0])
    hkv = None
    kvh = None
    for layer in range(depth):
        if layer < n_a:
            i = layer
            z, vsum, vsq = _mm_in(h, ssq, a_w_in, i)
            y = _gate(z, vsum, vsq, a_ln_g[i], a_ln_b[i], a_w_s[i], a_b_s[i])
            xs, (h,), ssq = _mm_res(y, a_w_out, i, xs, [ffn_norm_g[layer]], 512)
        else:
            i = layer - n_a
            qh = _mm_heads(h, ssq, b_w_q, i)
            o = _stick_attention(qh, kvh)
            xs, (h,), ssq = _mm_res(o, b_w_out, i, xs, [ffn_norm_g[layer]], 512)
        a = _ffn_up(h, ssq, ffn_w_up, layer, ffn_conv_w[layer], ffn_conv_b[layer])
        if layer == n_a - 1:
            xs, (hkv, h), ssq = _mm_res(a, ffn_w_down, layer, xs, [kv_norm_g, b_norm_g[0]], 256)
            kvh = _mm_heads(hkv, ssq, w_kv[None], 0)
        elif layer == depth - 1:
            xs, _, ssq = _mm_res(a, ffn_w_down, layer, xs, [], 256)
        else:
            nxt = a_norm_g[layer + 1] if layer + 1 < n_a else b_norm_g[layer + 1 - n_a]
            xs, (h,), ssq = _mm_res(a, ffn_w_down, layer, xs, [nxt], 256)
    return _final_norm(xs, final_norm_g)[None]
```

```python
import functools

import jax
import jax.numpy as jnp
from jax import lax
from jax.experimental import pallas as pl
from jax.experimental.pallas import tpu as pltpu

EPS = 1e-6
CHUNK = 64
HEAD_DIM = 128
LANES = 128
SUBLANES = 8
MXU_COLS = 256
VMEM_LIMIT_BYTES = 56 * 1024 * 1024
ROW_TILE = 1024
FFN_EPILOGUE_ROWS = 256
KEY_BLOCK = 128
STICK_CHAINS = 16
LOG_F32_TINY = -104.0

F32 = jnp.float32
BF16 = jnp.bfloat16


def _pick_tile(n, target):
    for align in (MXU_COLS, LANES):
        best = None
        t = align
        while t <= min(n, target):
            if n % t == 0:
                best = t
            t += align
        if best is not None:
            return best
    raise ValueError(f"no lane-aligned tile for {n}")


def _params(*sem):
    return pltpu.CompilerParams(dimension_semantics=sem, vmem_limit_bytes=VMEM_LIMIT_BYTES)


def _bf16(w):
    return w if w.dtype == BF16 else w.astype(BF16)


def _fold_lanes(a):
    out = a[:, 0:LANES]
    for k in range(1, a.shape[1] // LANES):
        out = out + a[:, k * LANES:(k + 1) * LANES]
    return out


def _inv_rms(ssq_ref, d_model):
    return lax.rsqrt(jnp.sum(ssq_ref[...], axis=1, keepdims=True) / d_model + EPS)


def _prep_kernel(x_ref, g_ref, h_ref, ssq_ref):
    x = x_ref[...]
    h_ref[...] = (x * g_ref[...]).astype(BF16)
    ssq_ref[...] = _fold_lanes(x * x)


def _prep(x, g):
    s, d = x.shape
    tm = min(256, s)
    return pl.pallas_call(
        _prep_kernel,
        grid=(s // tm,),
        in_specs=[pl.BlockSpec((tm, d), lambda i: (i, 0)),
                  pl.BlockSpec((1, d), lambda i: (0, 0))],
        out_specs=[pl.BlockSpec((tm, d), lambda i: (i, 0)),
                   pl.BlockSpec((tm, LANES), lambda i: (i, 0))],
        out_shape=[jax.ShapeDtypeStruct((s, d), BF16),
                   jax.ShapeDtypeStruct((s, LANES), F32)],
        compiler_params=_params("parallel"),
        name="prep",
    )(x, g.reshape(1, d))


def _final_norm_kernel(x_ref, g_ref, o_ref):
    x = x_ref[...]
    inv = lax.rsqrt(jnp.mean(x * x, axis=1, keepdims=True) + EPS)
    o_ref[...] = x * inv * g_ref[...]


def _final_norm(x, g):
    s, d = x.shape
    tm = min(256, s)
    return pl.pallas_call(
        _final_norm_kernel,
        grid=(s // tm,),
        in_specs=[pl.BlockSpec((tm, d), lambda i: (i, 0)),
                  pl.BlockSpec((1, d), lambda i: (0, 0))],
        out_specs=pl.BlockSpec((tm, d), lambda i: (i, 0)),
        out_shape=jax.ShapeDtypeStruct((s, d), F32),
        compiler_params=_params("parallel"),
        name="final_norm",
    )(x, g.reshape(1, d))


def _mm_in_kernel(h_ref, ssq_ref, w_ref, z_ref, vsum_ref, vsq_ref, rs_ref, *, d_model, nj_half):
    j = pl.program_id(1)

    @pl.when(j == 0)
    def _():
        rs_ref[...] = _inv_rms(ssq_ref, d_model)

    acc = jnp.dot(h_ref[...], _bf16(w_ref[...]), preferred_element_type=F32)
    t = acc * rs_ref[...]
    z = 0.5 * t * (1.0 + lax.erf(t * (2.0 ** -0.5)))
    z_ref[...] = z.astype(BF16)

    @pl.when(j == nj_half)
    def _():
        vsum_ref[...] = jnp.zeros_like(vsum_ref)
        vsq_ref[...] = jnp.zeros_like(vsq_ref)

    @pl.when(j >= nj_half)
    def _():
        vsum_ref[...] += _fold_lanes(z)
        vsq_ref[...] += _fold_lanes(z * z)


def _mm_in(h, ssq, w_stack, layer):
    s, d = h.shape
    n = w_stack.shape[2]
    tm = min(ROW_TILE, s)
    tn = _pick_tile(n // 2, 512)
    nj = n // tn
    kern = functools.partial(_mm_in_kernel, d_model=d, nj_half=nj // 2)
    return pl.pallas_call(
        kern,
        grid=(s // tm, nj),
        in_specs=[pl.BlockSpec((tm, d), lambda i, j: (i, 0)),
                  pl.BlockSpec((tm, LANES), lambda i, j: (i, 0)),
                  pl.BlockSpec((None, d, tn), lambda i, j: (layer, 0, j))],
        out_specs=[pl.BlockSpec((tm, tn), lambda i, j: (i, j)),
                   pl.BlockSpec((tm, LANES), lambda i, j: (i, 0)),
                   pl.BlockSpec((tm, LANES), lambda i, j: (i, 0))],
        out_shape=[jax.ShapeDtypeStruct((s, n), BF16),
                   jax.ShapeDtypeStruct((s, LANES), F32),
                   jax.ShapeDtypeStruct((s, LANES), F32)],
        scratch_shapes=[pltpu.VMEM((tm, 1), F32)],
        compiler_params=_params("parallel", "arbitrary"),
        name="gmlp_in",
    )(h, ssq, w_stack)


def _gate_kernel(u_ref, v_ref, vsum_ref, vsq_ref, lng_ref, lnb_ref, ws_ref, bst_ref, y_ref,
                 *, width, n_groups, blk):
    mu = jnp.sum(vsum_ref[...], axis=1, keepdims=True) / width
    var = jnp.sum(vsq_ref[...], axis=1, keepdims=True) / width - mu * mu
    rstd = lax.rsqrt(var + EPS)
    vn = ((v_ref[...].astype(F32) - mu) * rstd * lng_ref[...] + lnb_ref[...]).astype(BF16)

    ri = lax.broadcasted_iota(jnp.int32, (blk, blk), 0)
    ci = lax.broadcasted_iota(jnp.int32, (blk, blk), 1)
    visible = (ci // CHUNK) <= (ri // CHUNK)
    gw = width // n_groups
    rows = u_ref.shape[0]
    for g in range(n_groups):
        wsg = jnp.where(visible, ws_ref[g], 0.0).astype(BF16)
        bias = bst_ref[:, g:g + 1]
        for n in range(rows // blk):
            r0 = n * blk
            sv = jnp.dot(wsg, vn[r0:r0 + blk, g * gw:(g + 1) * gw],
                         preferred_element_type=F32) + bias
            u = u_ref[r0:r0 + blk, g * gw:(g + 1) * gw].astype(F32)
            y_ref[r0:r0 + blk, g * gw:(g + 1) * gw] = (u * sv).astype(BF16)


def _gate(z, vsum, vsq, ln_g, ln_b, w_s, b_s):
    s = z.shape[0]
    width = z.shape[1] // 2
    n_groups, blk, _ = w_s.shape
    tb = min(512, s)
    kern = functools.partial(_gate_kernel, width=width, n_groups=n_groups, blk=blk)
    return pl.pallas_call(
        kern,
        grid=(s // tb,),
        in_specs=[pl.BlockSpec((tb, width), lambda i: (i, 0)),
                  pl.BlockSpec((tb, width), lambda i: (i, 1)),
                  pl.BlockSpec((tb, LANES), lambda i: (i, 0)),
                  pl.BlockSpec((tb, LANES), lambda i: (i, 0)),
                  pl.BlockSpec((1, width), lambda i: (0, 0)),
                  pl.BlockSpec((1, width), lambda i: (0, 0)),
                  pl.BlockSpec((n_groups, blk, blk), lambda i: (0, 0, 0)),
                  pl.BlockSpec((blk, n_groups), lambda i: (0, 0))],
        out_specs=pl.BlockSpec((tb, width), lambda i: (i, 0)),
        out_shape=jax.ShapeDtypeStruct((s, width), BF16),
        compiler_params=_params("parallel"),
        name="gmlp_gate",
    )(z, z, vsum, vsq, ln_g.reshape(1, width), ln_b.reshape(1, width), w_s, jnp.transpose(b_s))


def _mm_res_kernel(*refs, n_gains):
    a_ref, w_ref, xr_ref = refs[:3]
    g_refs = refs[3:3 + n_gains]
    x_out = refs[3 + n_gains]
    hn_outs = refs[4 + n_gains:4 + 2 * n_gains]
    ssq_out = refs[4 + 2 * n_gains]
    j = pl.program_id(1)

    acc = jnp.dot(a_ref[...], _bf16(w_ref[...]), preferred_element_type=F32)
    xn = xr_ref[...] + acc
    x_out[...] = xn
    for g_ref, hn in zip(g_refs, hn_outs):
        hn[...] = (xn * g_ref[...]).astype(BF16)
    part = _fold_lanes(xn * xn)

    @pl.when(j == 0)
    def _():
        ssq_out[...] = part

    @pl.when(j > 0)
    def _():
        ssq_out[...] += part


def _mm_res(a, w_stack, layer, x_res, gains, tn_target):
    s, k = a.shape
    n = w_stack.shape[2]
    tm = min(ROW_TILE, s)
    tn = _pick_tile(n, tn_target)
    n_gains = len(gains)
    a_mode = {} if 2 * tm * k * 2 <= 16 * 1024 * 1024 else {"pipeline_mode": pl.Buffered(1)}
    kern = functools.partial(_mm_res_kernel, n_gains=n_gains)
    outs = pl.pallas_call(
        kern,
        grid=(s // tm, n // tn),
        in_specs=[pl.BlockSpec((tm, k), lambda i, j: (i, 0), **a_mode),
                  pl.BlockSpec((None, k, tn), lambda i, j: (layer, 0, j)),
                  pl.BlockSpec((tm, tn), lambda i, j: (i, j))]
                 + [pl.BlockSpec((1, tn), lambda i, j: (0, j))] * n_gains,
        out_specs=[pl.BlockSpec((tm, tn), lambda i, j: (i, j))] * (1 + n_gains)
                  + [pl.BlockSpec((tm, LANES), lambda i, j: (i, 0))],
        out_shape=[jax.ShapeDtypeStruct((s, n), F32)]
                  + [jax.ShapeDtypeStruct((s, n), BF16)] * n_gains
                  + [jax.ShapeDtypeStruct((s, LANES), F32)],
        compiler_params=_params("parallel", "arbitrary"),
        name="mm_res",
    )(a, w_stack, x_res, *[g.reshape(1, n) for g in gains])
    return outs[0], list(outs[1:1 + n_gains]), outs[1 + n_gains]


def _ffn_up_kernel(h_ref, ssq_ref, wg_ref, wv_ref, cwg_ref, cwv_ref, cbg_ref, cbv_ref,
                   a_ref, z_ref, rs_ref, tail_ref, *, d_model):
    i = pl.program_id(0)
    j = pl.program_id(1)
    tm, tn = a_ref.shape
    halo = SUBLANES

    @pl.when(j == 0)
    def _():
        rs_ref[...] = _inv_rms(ssq_ref, d_model)

    @pl.when((i == 0) & (j == 0))
    def _():
        tail_ref[...] = jnp.zeros_like(tail_ref)

    w = jnp.concatenate([_bf16(wg_ref[...]), _bf16(wv_ref[...])], axis=1)
    cw = jnp.concatenate([cwg_ref[...], cwv_ref[...]], axis=1)
    cb = jnp.concatenate([cbg_ref[...], cbv_ref[...]], axis=1)
    z_ref[0:halo, :] = jnp.where(i > 0, tail_ref[j], 0.0)
    z_ref[halo:, :] = jnp.dot(h_ref[...], w, preferred_element_type=F32) * rs_ref[...]
    tail_ref[j] = z_ref[tm:tm + halo, :]

    rows = min(FFN_EPILOGUE_ROWS, tm)
    for r0 in range(0, tm, rows):
        conv = (cw[0:1] * z_ref[halo - 2 + r0:halo - 2 + r0 + rows, :]
                + cw[1:2] * z_ref[halo - 1 + r0:halo - 1 + r0 + rows, :]
                + cw[2:3] * z_ref[halo + r0:halo + r0 + rows, :] + cb)
        half_gate = 0.5 * conv[:, :tn]
        a_ref[r0:r0 + rows, :] = ((half_gate + half_gate * jnp.tanh(half_gate))
                                  * conv[:, tn:]).astype(BF16)


def _ffn_up(h, ssq, w_up_stack, layer, conv_w, conv_b):
    s, d = h.shape
    d_ff = w_up_stack.shape[2] // 2
    assert conv_w.shape[0] == 3
    tm = min(ROW_TILE, s)
    tn = _pick_tile(d_ff, 256)
    nj = d_ff // tn
    kern = functools.partial(_ffn_up_kernel, d_model=d)
    cb = conv_b.reshape(1, 2 * d_ff)
    return pl.pallas_call(
        kern,
        grid=(s // tm, nj),
        in_specs=[pl.BlockSpec((tm, d), lambda i, j: (i, 0)),
                  pl.BlockSpec((tm, LANES), lambda i, j: (i, 0)),
                  pl.BlockSpec((None, d, tn), lambda i, j: (layer, 0, j)),
                  pl.BlockSpec((None, d, tn), lambda i, j: (layer, 0, j + nj)),
                  pl.BlockSpec((3, tn), lambda i, j: (0, j)),
                  pl.BlockSpec((3, tn), lambda i, j: (0, j + nj)),
                  pl.BlockSpec((1, tn), lambda i, j: (0, j)),
                  pl.BlockSpec((1, tn), lambda i, j: (0, j + nj))],
        out_specs=pl.BlockSpec((tm, tn), lambda i, j: (i, j)),
        out_shape=jax.ShapeDtypeStruct((s, d_ff), BF16),
        scratch_shapes=[pltpu.VMEM((SUBLANES + tm, 2 * tn), F32),
                        pltpu.VMEM((tm, 1), F32),
                        pltpu.VMEM((nj, SUBLANES, 2 * tn), F32)],
        compiler_params=_params("arbitrary", "arbitrary"),
        name="ffn_up",
    )(h, ssq, w_up_stack, w_up_stack, conv_w, conv_w, cb, cb)


def _mm_heads_kernel(h_ref, ssq_ref, w_ref, o_ref, rs_ref, *, d_model):
    @pl.when(pl.program_id(1) == 0)
    def _():
        rs_ref[...] = _inv_rms(ssq_ref, d_model)

    acc = jnp.dot(h_ref[...], _bf16(w_ref[...]), preferred_element_type=F32) * rs_ref[...]
    for hh in range(o_ref.shape[0]):
        o_ref[hh] = acc[:, hh * HEAD_DIM:(hh + 1) * HEAD_DIM].astype(BF16)


def _mm_heads(h, ssq, w_stack, layer):
    s, d = h.shape
    n = w_stack.shape[2]
    tm = min(ROW_TILE, s)
    tn = _pick_tile(n, 512)
    hpt = tn // HEAD_DIM
    kern = functools.partial(_mm_heads_kernel, d_model=d)
    return pl.pallas_call(
        kern,
        grid=(s // tm, n // tn),
        in_specs=[pl.BlockSpec((tm, d), lambda i, j: (i, 0)),
                  pl.BlockSpec((tm, LANES), lambda i, j: (i, 0)),
                  pl.BlockSpec((None, d, tn), lambda i, j: (layer, 0, j))],
        out_specs=pl.BlockSpec((hpt, tm, HEAD_DIM), lambda i, j: (j, i, 0)),
        out_shape=jax.ShapeDtypeStruct((n // HEAD_DIM, s, HEAD_DIM), BF16),
        scratch_shapes=[pltpu.VMEM((tm, 1), F32)],
        compiler_params=_params("parallel", "arbitrary"),
        name="mm_heads",
    )(h, ssq, w_stack)


def _stick_kernel(q_ref, k_ref, v_ref, o_ref, decay_ref, acc_ref, *, scale, n_chains):
    tq = KEY_BLOCK
    bk = KEY_BLOCK
    n_groups = q_ref.shape[1] // (tq * n_chains)

    kj = lax.broadcasted_iota(jnp.int32, (2 * bk, 2 * bk), 0) % bk
    ks = lax.broadcasted_iota(jnp.int32, (2 * bk, 2 * bk), 1)
    suffix = jnp.where((kj > ks) | (ks >= bk), 1.0, 0.0).astype(BF16)
    r_io = lax.broadcasted_iota(jnp.int32, (tq, bk), 0)
    c_io = lax.broadcasted_iota(jnp.int32, (tq, bk), 1)
    causal = c_io < r_io

    def sweep(qi0, dist, mode):
        chains = range(n_chains)
        kbs = [qi0 + c - dist for c in chains]
        k0s = [pl.multiple_of(jnp.maximum(kb, 0) * bk, bk) for kb in kbs]

        def masked(c, x):
            if mode == "full":
                return x
            return jnp.where(causal if mode == "diagonal" else kbs[c] >= 0, x, 0.0)

        zs = []
        for c in chains:
            q = q_ref[0, pl.ds(pl.multiple_of((qi0 + c) * tq, tq), tq), :]
            k = k_ref[0, pl.ds(k0s[c], bk), :]
            zs.append(lax.dot_general(q, k, (((1,), (1,)), ((), ())),
                                      preferred_element_type=F32) * scale)
        sps = [masked(c, jnp.maximum(zs[c], 0.0) + jnp.log(1.0 + jnp.exp(-jnp.abs(zs[c]))))
               for c in chains]
        sums = []
        for c in chains:
            hi = sps[c].astype(BF16)
            lo = (sps[c] - hi.astype(F32)).astype(BF16)
            sums.append(jnp.dot(jnp.concatenate([hi, lo], axis=1), suffix,
                                preferred_element_type=F32))
        weights = [masked(c, jnp.exp(zs[c] - sps[c] - (decay_ref[c] + sums[c][:, :bk])))
                   for c in chains]
        decays = []
        for c in chains:
            v = v_ref[0, pl.ds(k0s[c], bk), :]
            acc_ref[c] += jnp.dot(weights[c].astype(BF16), v, preferred_element_type=F32)
            decays.append(decay_ref[c] + sums[c][:, bk:])
            decay_ref[c] = decays[c]
        return decays

    def any_live(decays, qi0, next_dist, all_have_blocks):
        m = None
        for c, dec in enumerate(decays):
            if not all_have_blocks:
                dec = jnp.where(qi0 + c - next_dist >= 0, dec, -2.0 * LOG_F32_TINY)
            m = dec if m is None else jnp.minimum(m, dec)
        return jnp.min(m) < -LOG_F32_TINY

    def group(g, carry):
        qi0 = g * n_chains
        decay_ref[...] = jnp.zeros_like(decay_ref)
        acc_ref[...] = jnp.zeros_like(acc_ref)
        decays = sweep(qi0, 0, "diagonal")

        def cond(state):
            return state[1]

        def body(state):
            dist = state[0]
            live = lax.cond(
                qi0 - dist - 1 >= 0,
                lambda: any_live(sweep(qi0, dist, "full"), qi0, dist + 1, True),
                lambda: any_live(sweep(qi0, dist, "partial"), qi0, dist + 1, False))
            return dist + 1, live

        lax.while_loop(cond, body, (jnp.int32(1), any_live(decays, qi0, 1, False)))
        for c in range(n_chains):
            o_ref[pl.ds(pl.multiple_of((qi0 + c) * tq, tq), tq), :] = acc_ref[c].astype(BF16)
        return carry

    lax.fori_loop(0, n_groups, group, 0)


def _stick_attention(qh, kvh):
    n_heads, s, _ = qh.shape
    n_chains = min(STICK_CHAINS, s // KEY_BLOCK)
    assert s % (KEY_BLOCK * n_chains) == 0
    kern = functools.partial(_stick_kernel, scale=HEAD_DIM ** -0.5, n_chains=n_chains)
    return pl.pallas_call(
        kern,
        grid=(n_heads,),
        in_specs=[pl.BlockSpec((1, s, HEAD_DIM), lambda h: (h, 0, 0)),
                  pl.BlockSpec((1, s, HEAD_DIM), lambda h: (h, 0, 0)),
                  pl.BlockSpec((1, s, HEAD_DIM), lambda h: (h + n_heads, 0, 0))],
        out_specs=pl.BlockSpec((s, HEAD_DIM), lambda h: (0, h)),
        out_shape=jax.ShapeDtypeStruct((s, n_heads * HEAD_DIM), BF16),
        scratch_shapes=[pltpu.VMEM((n_chains, KEY_BLOCK, KEY_BLOCK), F32),
                        pltpu.VMEM((n_chains, KEY_BLOCK, HEAD_DIM), F32)],
        compiler_params=_params("parallel"),
        name="stick_attention",
    )(qh, kvh, kvh)


def kernel(x, a_norm_g, a_w_in, a_ln_g, a_ln_b, a_w_s, a_b_s, a_w_out, b_norm_g, b_w_q, b_w_out,
           kv_norm_g, w_kv, ffn_norm_g, ffn_w_up, ffn_conv_w, ffn_conv_b, ffn_w_down, final_norm_g):
    bsz, s, d = x.shape
    assert bsz == 1
    n_a = a_w_in.shape[0]
    n_b = b_w_q.shape[0]
    depth = n_a + n_b
    xs = x[0]
    h, ssq = _prep(xs, a_norm_g[0])
    hkv = None
    kvh = None
    for layer in range(depth):
        if layer < n_a:
            i = layer
            z, vsum, vsq = _mm_in(h, ssq, a_w_in, i)
            y = _gate(z, vsum, vsq, a_ln_g[i], a_ln_b[i], a_w_s[i], a_b_s[i])
            xs, (h,), ssq = _mm_res(y, a_w_out, i, xs, [ffn_norm_g[layer]], 512)
        else:
            i = layer - n_a
            qh = _mm_heads(h, ssq, b_w_q, i)
            o = _stick_attention(qh, kvh)
            xs, (h,), ssq = _mm_res(o, b_w_out, i, xs, [ffn_norm_g[layer]], 512)
        a = _ffn_up(h, ssq, ffn_w_up, layer, ffn_conv_w[layer], ffn_conv_b[layer])
        if layer == n_a - 1:
            xs, (hkv, h), ssq = _mm_res(a, ffn_w_down, layer, xs, [kv_norm_g, b_norm_g[0]], 256)
            kvh = _mm_heads(hkv, ssq, w_kv[None], 0)
        elif layer == depth - 1:
            xs, _, ssq = _mm_res(a, ffn_w_down, layer, xs, [], 256)
        else:
            nxt = a_norm_g[layer + 1] if layer + 1 < n_a else b_norm_g[layer + 1 - n_a]
            xs, (h,), ssq = _mm_res(a, ffn_w_down, layer, xs, [nxt], 256)
    return _final_norm(xs, final_norm_g)[None]
```

```python
import functools

import jax
import jax.numpy as jnp
from jax import lax
from jax.experimental import pallas as pl
from jax.experimental.pallas import tpu as pltpu

EPS = 1e-6
CHUNK = 64
HEAD_DIM = 128
LANES = 128
SUBLANES = 8
MXU_COLS = 256
VMEM_LIMIT_BYTES = 56 * 1024 * 1024
ROW_TILE = 1024
FFN_EPILOGUE_ROWS = 256
KEY_BLOCK = 128
STICK_CHAINS = 16
LOG_F32_TINY = -104.0

F32 = jnp.float32
BF16 = jnp.bfloat16


def _pick_tile(n, target):
    for align in (MXU_COLS, LANES):
        best = None
        t = align
        while t <= min(n, target):
            if n % t == 0:
                best = t
            t += align
        if best is not None:
            return best
    raise ValueError(f"no lane-aligned tile for {n}")


def _params(*sem):
    return pltpu.CompilerParams(dimension_semantics=sem, vmem_limit_bytes=VMEM_LIMIT_BYTES)


def _bf16(w):
    return w if w.dtype == BF16 else w.astype(BF16)


def _fold_lanes(a):
    out = a[:, 0:LANES]
    for k in range(1, a.shape[1] // LANES):
        out = out + a[:, k * LANES:(k + 1) * LANES]
    return out


def _inv_rms(ssq_ref, d_model):
    return lax.rsqrt(jnp.sum(ssq_ref[...], axis=1, keepdims=True) / d_model + EPS)


def _prep_kernel(x_ref, g_ref, h_ref, ssq_ref):
    x = x_ref[...]
    h_ref[...] = (x * g_ref[...]).astype(BF16)
    ssq_ref[...] = _fold_lanes(x * x)


def _prep(x, g):
    s, d = x.shape
    tm = min(256, s)
    return pl.pallas_call(
        _prep_kernel,
        grid=(s // tm,),
        in_specs=[pl.BlockSpec((tm, d), lambda i: (i, 0)),
                  pl.BlockSpec((1, d), lambda i: (0, 0))],
        out_specs=[pl.BlockSpec((tm, d), lambda i: (i, 0)),
                   pl.BlockSpec((tm, LANES), lambda i: (i, 0))],
        out_shape=[jax.ShapeDtypeStruct((s, d), BF16),
                   jax.ShapeDtypeStruct((s, LANES), F32)],
        compiler_params=_params("parallel"),
        name="prep",
    )(x, g.reshape(1, d))


def _final_norm_kernel(x_ref, g_ref, o_ref):
    x = x_ref[...]
    inv = lax.rsqrt(jnp.mean(x * x, axis=1, keepdims=True) + EPS)
    o_ref[...] = x * inv * g_ref[...]


def _final_norm(x, g):
    s, d = x.shape
    tm = min(256, s)
    return pl.pallas_call(
        _final_norm_kernel,
        grid=(s // tm,),
        in_specs=[pl.BlockSpec((tm, d), lambda i: (i, 0)),
                  pl.BlockSpec((1, d), lambda i: (0, 0))],
        out_specs=pl.BlockSpec((tm, d), lambda i: (i, 0)),
        out_shape=jax.ShapeDtypeStruct((s, d), F32),
        compiler_params=_params("parallel"),
        name="final_norm",
    )(x, g.reshape(1, d))


def _mm_in_kernel(h_ref, ssq_ref, w_ref, z_ref, vsum_ref, vsq_ref, rs_ref, *, d_model, nj_half):
    j = pl.program_id(1)

    @pl.when(j == 0)
    def _():
        rs_ref[...] = _inv_rms(ssq_ref, d_model)

    acc = jnp.dot(h_ref[...], _bf16(w_ref[...]), preferred_element_type=F32)
    t = acc * rs_ref[...]
    z = 0.5 * t * (1.0 + lax.erf(t * (2.0 ** -0.5)))
    z_ref[...] = z.astype(BF16)

    @pl.when(j == nj_half)
    def _():
        vsum_ref[...] = jnp.zeros_like(vsum_ref)
        vsq_ref[...] = jnp.zeros_like(vsq_ref)

    @pl.when(j >= nj_half)
    def _():
        vsum_ref[...] += _fold_lanes(z)
        vsq_ref[...] += _fold_lanes(z * z)


def _mm_in(h, ssq, w_stack, layer):
    s, d = h.shape
    n = w_stack.shape[2]
    tm = min(ROW_TILE, s)
    tn = _pick_tile(n // 2, 512)
    nj = n // tn
    kern = functools.partial(_mm_in_kernel, d_model=d, nj_half=nj // 2)
    return pl.pallas_call(
        kern,
        grid=(s // tm, nj),
        in_specs=[pl.BlockSpec((tm, d), lambda i, j: (i, 0)),
                  pl.BlockSpec((tm, LANES), lambda i, j: (i, 0)),
                  pl.BlockSpec((None, d, tn), lambda i, j: (layer, 0, j))],
        out_specs=[pl.BlockSpec((tm, tn), lambda i, j: (i, j)),
                   pl.BlockSpec((tm, LANES), lambda i, j: (i, 0)),
                   pl.BlockSpec((tm, LANES), lambda i, j: (i, 0))],
        out_shape=[jax.ShapeDtypeStruct((s, n), BF16),
                   jax.ShapeDtypeStruct((s, LANES), F32),
                   jax.ShapeDtypeStruct((s, LANES), F32)],
        scratch_shapes=[pltpu.VMEM((tm, 1), F32)],
        compiler_params=_params("parallel", "arbitrary"),
        name="gmlp_in",
    )(h, ssq, w_stack)


def _gate_kernel(u_ref, v_ref, vsum_ref, vsq_ref, lng_ref, lnb_ref, ws_ref, bst_ref, y_ref,
                 *, width, n_groups, blk):
    mu = jnp.sum(vsum_ref[...], axis=1, keepdims=True) / width
    var = jnp.sum(vsq_ref[...], axis=1, keepdims=True) / width - mu * mu
    rstd = lax.rsqrt(var + EPS)
    vn = ((v_ref[...].astype(F32) - mu) * rstd * lng_ref[...] + lnb_ref[...]).astype(BF16)

    ri = lax.broadcasted_iota(jnp.int32, (blk, blk), 0)
    ci = lax.broadcasted_iota(jnp.int32, (blk, blk), 1)
    visible = (ci // CHUNK) <= (ri // CHUNK)
    gw = width // n_groups
    rows = u_ref.shape[0]
    for g in range(n_groups):
        wsg = jnp.where(visible, ws_ref[g], 0.0).astype(BF16)
        bias = bst_ref[:, g:g + 1]
        for n in range(rows // blk):
            r0 = n * blk
            sv = jnp.dot(wsg, vn[r0:r0 + blk, g * gw:(g + 1) * gw],
                         preferred_element_type=F32) + bias
            u = u_ref[r0:r0 + blk, g * gw:(g + 1) * gw].astype(F32)
            y_ref[r0:r0 + blk, g * gw:(g + 1) * gw] = (u * sv).astype(BF16)


def _gate(z, vsum, vsq, ln_g, ln_b, w_s, b_s):
    s = z.shape[0]
    width = z.shape[1] // 2
    n_groups, blk, _ = w_s.shape
    tb = min(512, s)
    kern = functools.partial(_gate_kernel, width=width, n_groups=n_groups, blk=blk)
    return pl.pallas_call(
        kern,
        grid=(s // tb,),
        in_specs=[pl.BlockSpec((tb, width), lambda i: (i, 0)),
                  pl.BlockSpec((tb, width), lambda i: (i, 1)),
                  pl.BlockSpec((tb, LANES), lambda i: (i, 0)),
                  pl.BlockSpec((tb, LANES), lambda i: (i, 0)),
                  pl.BlockSpec((1, width), lambda i: (0, 0)),
                  pl.BlockSpec((1, width), lambda i: (0, 0)),
                  pl.BlockSpec((n_groups, blk, blk), lambda i: (0, 0, 0)),
                  pl.BlockSpec((blk, n_groups), lambda i: (0, 0))],
        out_specs=pl.BlockSpec((tb, width), lambda i: (i, 0)),
        out_shape=jax.ShapeDtypeStruct((s, width), BF16),
        compiler_params=_params("parallel"),
        name="gmlp_gate",
    )(z, z, vsum, vsq, ln_g.reshape(1, width), ln_b.reshape(1, width), w_s, jnp.transpose(b_s))


def _mm_res_kernel(*refs, n_gains):
    a_ref, w_ref, xr_ref = refs[:3]
    g_refs = refs[3:3 + n_gains]
    x_out = refs[3 + n_gains]
    hn_outs = refs[4 + n_gains:4 + 2 * n_gains]
    ssq_out = refs[4 + 2 * n_gains]
    j = pl.program_id(1)

    acc = jnp.dot(a_ref[...], _bf16(w_ref[...]), preferred_element_type=F32)
    xn = xr_ref[...] + acc
    x_out[...] = xn
    for g_ref, hn in zip(g_refs, hn_outs):
        hn[...] = (xn * g_ref[...]).astype(BF16)
    part = _fold_lanes(xn * xn)

    @pl.when(j == 0)
    def _():
        ssq_out[...] = part

    @pl.when(j > 0)
    def _():
        ssq_out[...] += part


def _mm_res(a, w_stack, layer, x_res, gains, tn_target):
    s, k = a.shape
    n = w_stack.shape[2]
    tm = min(ROW_TILE, s)
    tn = _pick_tile(n, tn_target)
    n_gains = len(gains)
    a_mode = {} if 2 * tm * k * 2 <= 16 * 1024 * 1024 else {"pipeline_mode": pl.Buffered(1)}
    kern = functools.partial(_mm_res_kernel, n_gains=n_gains)
    outs = pl.pallas_call(
        kern,
        grid=(s // tm, n // tn),
        in_specs=[pl.BlockSpec((tm, k), lambda i, j: (i, 0), **a_mode),
                  pl.BlockSpec((None, k, tn), lambda i, j: (layer, 0, j)),
                  pl.BlockSpec((tm, tn), lambda i, j: (i, j))]
                 + [pl.BlockSpec((1, tn), lambda i, j: (0, j))] * n_gains,
        out_specs=[pl.BlockSpec((tm, tn), lambda i, j: (i, j))] * (1 + n_gains)
                  + [pl.BlockSpec((tm, LANES), lambda i, j: (i, 0))],
        out_shape=[jax.ShapeDtypeStruct((s, n), F32)]
                  + [jax.ShapeDtypeStruct((s, n), BF16)] * n_gains
                  + [jax.ShapeDtypeStruct((s, LANES), F32)],
        compiler_params=_params("parallel", "arbitrary"),
        name="mm_res",
    )(a, w_stack, x_res, *[g.reshape(1, n) for g in gains])
    return outs[0], list(outs[1:1 + n_gains]), outs[1 + n_gains]


def _shift_rows(z, prev, shift):
    row = lax.broadcasted_iota(jnp.int32, prev.shape, 0)
    top = jnp.where(row < shift, pltpu.roll(prev, shift, 0), pltpu.roll(z[0:SUBLANES], shift, 0))
    return jnp.concatenate([top, pltpu.roll(z, shift, 0)[SUBLANES:]], axis=0)


def _ffn_up_kernel(h_ref, ssq_ref, wg_ref, wv_ref, cwg_ref, cwv_ref, cbg_ref, cbv_ref,
                   a_ref, z_ref, rs_ref, tail_ref, *, d_model):
    i = pl.program_id(0)
    j = pl.program_id(1)
    tm, tn = a_ref.shape
    halo = SUBLANES

    @pl.when(j == 0)
    def _():
        rs_ref[...] = _inv_rms(ssq_ref, d_model)

    @pl.when((i == 0) & (j == 0))
    def _():
        tail_ref[...] = jnp.zeros_like(tail_ref)

    w = jnp.concatenate([_bf16(wg_ref[...]), _bf16(wv_ref[...])], axis=1)
    cw = jnp.concatenate([0.5 * cwg_ref[...], cwv_ref[...]], axis=1)
    cb = jnp.concatenate([0.5 * cbg_ref[...], cbv_ref[...]], axis=1)
    z_ref[0:halo, :] = jnp.where(i > 0, tail_ref[j], 0.0)
    z_ref[halo:, :] = jnp.dot(h_ref[...], w, preferred_element_type=F32) * rs_ref[...]
    tail_ref[j] = z_ref[tm:tm + halo, :]

    rows = min(FFN_EPILOGUE_ROWS, tm)
    for r0 in range(0, tm, rows):
        prev = z_ref[r0:r0 + halo, :]
        zc = z_ref[halo + r0:halo + r0 + rows, :]
        conv = (cw[0:1] * _shift_rows(zc, prev, 2) + cw[1:2] * _shift_rows(zc, prev, 1)
                + cw[2:3] * zc + cb)
        half_gate = conv[:, :tn]
        a_ref[r0:r0 + rows, :] = ((half_gate + half_gate * jnp.tanh(half_gate))
                                  * conv[:, tn:]).astype(BF16)


def _ffn_up(h, ssq, w_up_stack, layer, conv_w, conv_b):
    s, d = h.shape
    d_ff = w_up_stack.shape[2] // 2
    assert conv_w.shape[0] == 3
    tm = min(ROW_TILE, s)
    tn = _pick_tile(d_ff, 256)
    nj = d_ff // tn
    kern = functools.partial(_ffn_up_kernel, d_model=d)
    cb = conv_b.reshape(1, 2 * d_ff)
    return pl.pallas_call(
        kern,
        grid=(s // tm, nj),
        in_specs=[pl.BlockSpec((tm, d), lambda i, j: (i, 0)),
                  pl.BlockSpec((tm, LANES), lambda i, j: (i, 0)),
                  pl.BlockSpec((None, d, tn), lambda i, j: (layer, 0, j)),
                  pl.BlockSpec((None, d, tn), lambda i, j: (layer, 0, j + nj)),
                  pl.BlockSpec((3, tn), lambda i, j: (0, j)),
                  pl.BlockSpec((3, tn), lambda i, j: (0, j + nj)),
                  pl.BlockSpec((1, tn), lambda i, j: (0, j)),
                  pl.BlockSpec((1, tn), lambda i, j: (0, j + nj))],
        out_specs=pl.BlockSpec((tm, tn), lambda i, j: (i, j)),
        out_shape=jax.ShapeDtypeStruct((s, d_ff), BF16),
        scratch_shapes=[pltpu.VMEM((SUBLANES + tm, 2 * tn), F32),
                        pltpu.VMEM((tm, 1), F32),
                        pltpu.VMEM((nj, SUBLANES, 2 * tn), F32)],
        compiler_params=_params("arbitrary", "arbitrary"),
        name="ffn_up",
    )(h, ssq, w_up_stack, w_up_stack, conv_w, conv_w, cb, cb)


def _mm_heads_kernel(h_ref, ssq_ref, w_ref, o_ref, rs_ref, *, d_model):
    @pl.when(pl.program_id(1) == 0)
    def _():
        rs_ref[...] = _inv_rms(ssq_ref, d_model)

    acc = jnp.dot(h_ref[...], _bf16(w_ref[...]), preferred_element_type=F32) * rs_ref[...]
    for hh in range(o_ref.shape[0]):
        o_ref[hh] = acc[:, hh * HEAD_DIM:(hh + 1) * HEAD_DIM].astype(BF16)


def _mm_heads(h, ssq, w_stack, layer):
    s, d = h.shape
    n = w_stack.shape[2]
    tm = min(ROW_TILE, s)
    tn = _pick_tile(n, 512)
    hpt = tn // HEAD_DIM
    kern = functools.partial(_mm_heads_kernel, d_model=d)
    return pl.pallas_call(
        kern,
        grid=(s // tm, n // tn),
        in_specs=[pl.BlockSpec((tm, d), lambda i, j: (i, 0)),
                  pl.BlockSpec((tm, LANES), lambda i, j: (i, 0)),
                  pl.BlockSpec((None, d, tn), lambda i, j: (layer, 0, j))],
        out_specs=pl.BlockSpec((hpt, tm, HEAD_DIM), lambda i, j: (j, i, 0)),
        out_shape=jax.ShapeDtypeStruct((n // HEAD_DIM, s, HEAD_DIM), BF16),
        scratch_shapes=[pltpu.VMEM((tm, 1), F32)],
        compiler_params=_params("parallel", "arbitrary"),
        name="mm_heads",
    )(h, ssq, w_stack)


def _stick_kernel(q_ref, k_ref, v_ref, o_ref, decay_ref, acc_ref, *, scale, n_chains):
    tq = KEY_BLOCK
    bk = KEY_BLOCK
    n_groups = q_ref.shape[1] // (tq * n_chains)

    kj = lax.broadcasted_iota(jnp.int32, (2 * bk, 2 * bk), 0) % bk
    ks = lax.broadcasted_iota(jnp.int32, (2 * bk, 2 * bk), 1)
    suffix = jnp.where((kj > ks) | (ks >= bk), 1.0, 0.0).astype(BF16)
    r_io = lax.broadcasted_iota(jnp.int32, (tq, bk), 0)
    c_io = lax.broadcasted_iota(jnp.int32, (tq, bk), 1)
    causal = c_io < r_io

    def sweep(qi0, dist, mode):
        chains = range(n_chains)
        kbs = [qi0 + c - dist for c in chains]
        k0s = [pl.multiple_of(jnp.maximum(kb, 0) * bk, bk) for kb in kbs]

        def masked(c, x):
            if mode == "full":
                return x
            return jnp.where(causal if mode == "diagonal" else kbs[c] >= 0, x, 0.0)

        zs = []
        for c in chains:
            q = q_ref[0, pl.ds(pl.multiple_of((qi0 + c) * tq, tq), tq), :]
            k = k_ref[0, pl.ds(k0s[c], bk), :]
            zs.append(lax.dot_general(q, k, (((1,), (1,)), ((), ())),
                                      preferred_element_type=F32) * scale)
        sps = [masked(c, jnp.maximum(zs[c], 0.0) + jnp.log(1.0 + jnp.exp(-jnp.abs(zs[c]))))
               for c in chains]
        sums = []
        for c in chains:
            hi = sps[c].astype(BF16)
            lo = (sps[c] - hi.astype(F32)).astype(BF16)
            sums.append(jnp.dot(jnp.concatenate([hi, lo], axis=1), suffix,
                                preferred_element_type=F32))
        weights = [masked(c, jnp.exp(zs[c] - sps[c] - (decay_ref[c] + sums[c][:, :bk])))
                   for c in chains]
        decays = []
        for c in chains:
            v = v_ref[0, pl.ds(k0s[c], bk), :]
            acc_ref[c] += jnp.dot(weights[c].astype(BF16), v, preferred_element_type=F32)
            decays.append(decay_ref[c] + sums[c][:, bk:])
            decay_ref[c] = decays[c]
        return decays

    def any_live(decays, qi0, next_dist, all_have_blocks):
        m = None
        for c, dec in enumerate(decays):
            if not all_have_blocks:
                dec = jnp.where(qi0 + c - next_dist >= 0, dec, -2.0 * LOG_F32_TINY)
            m = dec if m is None else jnp.minimum(m, dec)
        return jnp.min(m) < -LOG_F32_TINY

    def group(g, carry):
        qi0 = g * n_chains
        decay_ref[...] = jnp.zeros_like(decay_ref)
        acc_ref[...] = jnp.zeros_like(acc_ref)
        decays = sweep(qi0, 0, "diagonal")

        def cond(state):
            return state[1]

        def body(state):
            dist = state[0]
            live = lax.cond(
                qi0 - dist - 1 >= 0,
                lambda: any_live(sweep(qi0, dist, "full"), qi0, dist + 1, True),
                lambda: any_live(sweep(qi0, dist, "partial"), qi0, dist + 1, False))
            return dist + 1, live

        lax.while_loop(cond, body, (jnp.int32(1), any_live(decays, qi0, 1, False)))
        for c in range(n_chains):
            o_ref[pl.ds(pl.multiple_of((qi0 + c) * tq, tq), tq), :] = acc_ref[c].astype(BF16)
        return carry

    lax.fori_loop(0, n_groups, group, 0)


def _stick_attention(qh, kvh):
    n_heads, s, _ = qh.shape
    n_chains = min(STICK_CHAINS, s // KEY_BLOCK)
    assert s % (KEY_BLOCK * n_chains) == 0
    kern = functools.partial(_stick_kernel, scale=HEAD_DIM ** -0.5, n_chains=n_chains)
    return pl.pallas_call(
        kern,
        grid=(n_heads,),
        in_specs=[pl.BlockSpec((1, s, HEAD_DIM), lambda h: (h, 0, 0)),
                  pl.BlockSpec((1, s, HEAD_DIM), lambda h: (h, 0, 0)),
                  pl.BlockSpec((1, s, HEAD_DIM), lambda h: (h + n_heads, 0, 0))],
        out_specs=pl.BlockSpec((s, HEAD_DIM), lambda h: (0, h)),
        out_shape=jax.ShapeDtypeStruct((s, n_heads * HEAD_DIM), BF16),
        scratch_shapes=[pltpu.VMEM((n_chains, KEY_BLOCK, KEY_BLOCK), F32),
                        pltpu.VMEM((n_chains, KEY_BLOCK, HEAD_DIM), F32)],
        compiler_params=_params("parallel"),
        name="stick_attention",
    )(qh, kvh, kvh)


def kernel(x, a_norm_g, a_w_in, a_ln_g, a_ln_b, a_w_s, a_b_s, a_w_out, b_norm_g, b_w_q, b_w_out,
           kv_norm_g, w_kv, ffn_norm_g, ffn_w_up, ffn_conv_w, ffn_conv_b, ffn_w_down, final_norm_g):
    bsz, s, d = x.shape
    assert bsz == 1
    n_a = a_w_in.shape[0]
    n_b = b_w_q.shape[0]
    depth = n_a + n_b
    xs = x[0]
    h, ssq = _prep(xs, a_norm_g[0])
    hkv = None
    kvh = None
    for layer in range(depth):
        if layer < n_a:
            i = layer
            z, vsum, vsq = _mm_in(h, ssq, a_w_in, i)
            y = _gate(z, vsum, vsq, a_ln_g[i], a_ln_b[i], a_w_s[i], a_b_s[i])
            xs, (h,), ssq = _mm_res(y, a_w_out, i, xs, [ffn_norm_g[layer]], 512)
        else:
            i = layer - n_a
            qh = _mm_heads(h, ssq, b_w_q, i)
            o = _stick_attention(qh, kvh)
            xs, (h,), ssq = _mm_res(o, b_w_out, i, xs, [ffn_norm_g[layer]], 512)
        a = _ffn_up(h, ssq, ffn_w_up, layer, ffn_conv_w[layer], ffn_conv_b[layer])
        if layer == n_a - 1:
            xs, (hkv, h), ssq = _mm_res(a, ffn_w_down, layer, xs, [kv_norm_g, b_norm_g[0]], 256)
            kvh = _mm_heads(hkv, ssq, w_kv[None], 0)
        elif layer == depth - 1:
            xs, _, ssq = _mm_res(a, ffn_w_down, layer, xs, [], 256)
        else:
            nxt = a_norm_g[layer + 1] if layer + 1 < n_a else b_norm_g[layer + 1 - n_a]
            xs, (h,), ssq = _mm_res(a, ffn_w_down, layer, xs, [nxt], 256)
    return _final_norm(xs, final_norm_g)[None]
```

```python
import functools

import jax
import jax.numpy as jnp
from jax import lax
from jax.experimental import pallas as pl
from jax.experimental.pallas import tpu as pltpu

EPS = 1e-6
CHUNK = 64
HEAD_DIM = 128
LANES = 128
SUBLANES = 8
MXU_COLS = 256
VMEM_LIMIT_BYTES = 56 * 1024 * 1024
ROW_TILE = 1024
FFN_EPILOGUE_ROWS = 256
KEY_BLOCK = 128
STICK_CHAINS = 16
LOG_F32_TINY = -104.0

F32 = jnp.float32
BF16 = jnp.bfloat16


def _pick_tile(n, target):
    for align in (MXU_COLS, LANES):
        best = None
        t = align
        while t <= min(n, target):
            if n % t == 0:
                best = t
            t += align
        if best is not None:
            return best
    raise ValueError(f"no lane-aligned tile for {n}")


def _params(*sem):
    return pltpu.CompilerParams(dimension_semantics=sem, vmem_limit_bytes=VMEM_LIMIT_BYTES)


def _bf16(w):
    return w if w.dtype == BF16 else w.astype(BF16)


def _fold_lanes(a):
    out = a[:, 0:LANES]
    for k in range(1, a.shape[1] // LANES):
        out = out + a[:, k * LANES:(k + 1) * LANES]
    return out


def _inv_rms(ssq_ref, d_model):
    return lax.rsqrt(jnp.sum(ssq_ref[...], axis=1, keepdims=True) / d_model + EPS)


def _prep_kernel(x_ref, g_ref, h_ref, ssq_ref):
    x = x_ref[...]
    h_ref[...] = (x * g_ref[...]).astype(BF16)
    ssq_ref[...] = _fold_lanes(x * x)


def _prep(x, g):
    s, d = x.shape
    tm = min(256, s)
    return pl.pallas_call(
        _prep_kernel,
        grid=(s // tm,),
        in_specs=[pl.BlockSpec((tm, d), lambda i: (i, 0)),
                  pl.BlockSpec((1, d), lambda i: (0, 0))],
        out_specs=[pl.BlockSpec((tm, d), lambda i: (i, 0)),
                   pl.BlockSpec((tm, LANES), lambda i: (i, 0))],
        out_shape=[jax.ShapeDtypeStruct((s, d), BF16),
                   jax.ShapeDtypeStruct((s, LANES), F32)],
        compiler_params=_params("parallel"),
        name="prep",
    )(x, g.reshape(1, d))


def _final_norm_kernel(x_ref, g_ref, o_ref):
    x = x_ref[...]
    inv = lax.rsqrt(jnp.mean(x * x, axis=1, keepdims=True) + EPS)
    o_ref[...] = x * inv * g_ref[...]


def _final_norm(x, g):
    s, d = x.shape
    tm = min(256, s)
    return pl.pallas_call(
        _final_norm_kernel,
        grid=(s // tm,),
        in_specs=[pl.BlockSpec((tm, d), lambda i: (i, 0)),
                  pl.BlockSpec((1, d), lambda i: (0, 0))],
        out_specs=pl.BlockSpec((tm, d), lambda i: (i, 0)),
        out_shape=jax.ShapeDtypeStruct((s, d), F32),
        compiler_params=_params("parallel"),
        name="final_norm",
    )(x, g.reshape(1, d))


def _mm_in_kernel(h_ref, ssq_ref, w_ref, z_ref, vsum_ref, vsq_ref, rs_ref, *, d_model, nj_half):
    j = pl.program_id(1)

    @pl.when(j == 0)
    def _():
        rs_ref[...] = _inv_rms(ssq_ref, d_model)

    acc = jnp.dot(h_ref[...], _bf16(w_ref[...]), preferred_element_type=F32)
    t = acc * rs_ref[...]
    z = 0.5 * t * (1.0 + lax.erf(t * (2.0 ** -0.5)))
    z_ref[...] = z.astype(BF16)

    @pl.when(j == nj_half)
    def _():
        vsum_ref[...] = jnp.zeros_like(vsum_ref)
        vsq_ref[...] = jnp.zeros_like(vsq_ref)

    @pl.when(j >= nj_half)
    def _():
        vsum_ref[...] += _fold_lanes(z)
        vsq_ref[...] += _fold_lanes(z * z)


def _mm_in(h, ssq, w_stack, layer):
    s, d = h.shape
    n = w_stack.shape[2]
    tm = min(ROW_TILE, s)
    tn = _pick_tile(n // 2, 512)
    nj = n // tn
    kern = functools.partial(_mm_in_kernel, d_model=d, nj_half=nj // 2)
    return pl.pallas_call(
        kern,
        grid=(s // tm, nj),
        in_specs=[pl.BlockSpec((tm, d), lambda i, j: (i, 0)),
                  pl.BlockSpec((tm, LANES), lambda i, j: (i, 0)),
                  pl.BlockSpec((None, d, tn), lambda i, j: (layer, 0, j))],
        out_specs=[pl.BlockSpec((tm, tn), lambda i, j: (i, j)),
                   pl.BlockSpec((tm, LANES), lambda i, j: (i, 0)),
                   pl.BlockSpec((tm, LANES), lambda i, j: (i, 0))],
        out_shape=[jax.ShapeDtypeStruct((s, n), BF16),
                   jax.ShapeDtypeStruct((s, LANES), F32),
                   jax.ShapeDtypeStruct((s, LANES), F32)],
        scratch_shapes=[pltpu.VMEM((tm, 1), F32)],
        compiler_params=_params("parallel", "arbitrary"),
        name="gmlp_in",
    )(h, ssq, w_stack)


def _gate_kernel(u_ref, v_ref, vsum_ref, vsq_ref, lng_ref, lnb_ref, ws_ref, bst_ref, y_ref,
                 *, width, n_groups, blk):
    mu = jnp.sum(vsum_ref[...], axis=1, keepdims=True) / width
    var = jnp.sum(vsq_ref[...], axis=1, keepdims=True) / width - mu * mu
    rstd = lax.rsqrt(var + EPS)
    vn = ((v_ref[...].astype(F32) - mu) * rstd * lng_ref[...] + lnb_ref[...]).astype(BF16)

    ri = lax.broadcasted_iota(jnp.int32, (blk, blk), 0)
    ci = lax.broadcasted_iota(jnp.int32, (blk, blk), 1)
    visible = (ci // CHUNK) <= (ri // CHUNK)
    gw = width // n_groups
    rows = u_ref.shape[0]
    for g in range(n_groups):
        wsg = jnp.where(visible, ws_ref[g], 0.0).astype(BF16)
        bias = bst_ref[:, g:g + 1]
        for n in range(rows // blk):
            r0 = n * blk
            sv = jnp.dot(wsg, vn[r0:r0 + blk, g * gw:(g + 1) * gw],
                         preferred_element_type=F32) + bias
            u = u_ref[r0:r0 + blk, g * gw:(g + 1) * gw].astype(F32)
            y_ref[r0:r0 + blk, g * gw:(g + 1) * gw] = (u * sv).astype(BF16)


def _gate(z, vsum, vsq, ln_g, ln_b, w_s, b_s):
    s = z.shape[0]
    width = z.shape[1] // 2
    n_groups, blk, _ = w_s.shape
    tb = min(512, s)
    kern = functools.partial(_gate_kernel, width=width, n_groups=n_groups, blk=blk)
    return pl.pallas_call(
        kern,
        grid=(s // tb,),
        in_specs=[pl.BlockSpec((tb, width), lambda i: (i, 0)),
                  pl.BlockSpec((tb, width), lambda i: (i, 1)),
                  pl.BlockSpec((tb, LANES), lambda i: (i, 0)),
                  pl.BlockSpec((tb, LANES), lambda i: (i, 0)),
                  pl.BlockSpec((1, width), lambda i: (0, 0)),
                  pl.BlockSpec((1, width), lambda i: (0, 0)),
                  pl.BlockSpec((n_groups, blk, blk), lambda i: (0, 0, 0)),
                  pl.BlockSpec((blk, n_groups), lambda i: (0, 0))],
        out_specs=pl.BlockSpec((tb, width), lambda i: (i, 0)),
        out_shape=jax.ShapeDtypeStruct((s, width), BF16),
        compiler_params=_params("parallel"),
        name="gmlp_gate",
    )(z, z, vsum, vsq, ln_g.reshape(1, width), ln_b.reshape(1, width), w_s, jnp.transpose(b_s))


def _mm_res_kernel(*refs, n_gains, rows_interleaved):
    a_ref, w_ref, xr_ref = refs[:3]
    g_refs = refs[3:3 + n_gains]
    x_out = refs[3 + n_gains]
    hn_outs = refs[4 + n_gains:4 + 2 * n_gains]
    ssq_out = refs[4 + 2 * n_gains]
    j = pl.program_id(1)

    acc = jnp.dot(a_ref[...], _bf16(w_ref[...]), preferred_element_type=F32)
    if rows_interleaved:
        perm_ref, nat_ref = refs[5 + 2 * n_gains:]
        tm = acc.shape[0]
        groups = tm // SUBLANES
        for c in range(perm_ref.shape[0]):
            perm_ref[c] = acc[:, c * LANES:(c + 1) * LANES]
            for u in range(groups):
                s_idx, v0 = divmod(u * SUBLANES, groups)
                nat_ref[u * SUBLANES:(u + 1) * SUBLANES, c * LANES:(c + 1) * LANES] = perm_ref[
                    c, pl.ds(v0 * SUBLANES + s_idx, SUBLANES, stride=SUBLANES), :]
        acc = nat_ref[...]
    xn = xr_ref[...] + acc
    x_out[...] = xn
    for g_ref, hn in zip(g_refs, hn_outs):
        hn[...] = (xn * g_ref[...]).astype(BF16)
    part = _fold_lanes(xn * xn)

    @pl.when(j == 0)
    def _():
        ssq_out[...] = part

    @pl.when(j > 0)
    def _():
        ssq_out[...] += part


def _mm_res(a, w_stack, layer, x_res, gains, tn_target, rows_interleaved=False):
    s, k = a.shape
    n = w_stack.shape[2]
    tm = min(ROW_TILE, s)
    tn = _pick_tile(n, tn_target)
    n_gains = len(gains)
    a_mode = {} if 2 * tm * k * 2 <= 16 * 1024 * 1024 else {"pipeline_mode": pl.Buffered(1)}
    kern = functools.partial(_mm_res_kernel, n_gains=n_gains, rows_interleaved=rows_interleaved)
    scratch = ([pltpu.VMEM((tn // LANES, tm, LANES), F32), pltpu.VMEM((tm, tn), F32)]
               if rows_interleaved else [])
    outs = pl.pallas_call(
        kern,
        grid=(s // tm, n // tn),
        in_specs=[pl.BlockSpec((tm, k), lambda i, j: (i, 0), **a_mode),
                  pl.BlockSpec((None, k, tn), lambda i, j: (layer, 0, j)),
                  pl.BlockSpec((tm, tn), lambda i, j: (i, j))]
                 + [pl.BlockSpec((1, tn), lambda i, j: (0, j))] * n_gains,
        out_specs=[pl.BlockSpec((tm, tn), lambda i, j: (i, j))] * (1 + n_gains)
                  + [pl.BlockSpec((tm, LANES), lambda i, j: (i, 0))],
        out_shape=[jax.ShapeDtypeStruct((s, n), F32)]
                  + [jax.ShapeDtypeStruct((s, n), BF16)] * n_gains
                  + [jax.ShapeDtypeStruct((s, LANES), F32)],
        scratch_shapes=scratch,
        compiler_params=_params("parallel", "arbitrary"),
        name="mm_res",
    )(a, w_stack, x_res, *[g.reshape(1, n) for g in gains])
    return outs[0], list(outs[1:1 + n_gains]), outs[1 + n_gains]


def _ffn_up_kernel(h_ref, ssq_ref, wg_ref, wv_ref, cwg_ref, cwv_ref, cbg_ref, cbv_ref,
                   a_ref, hp_ref, z_ref, ssqp_ref, rsp_ref, tail_ref, *, d_model):
    i = pl.program_id(0)
    j = pl.program_id(1)
    tm, tn = a_ref.shape
    d = h_ref.shape[1]
    groups = tm // SUBLANES
    halo = 2 * SUBLANES

    @pl.when(j == 0)
    def _():
        for v in range(groups):
            ssqp_ref[v * SUBLANES:(v + 1) * SUBLANES, :] = ssq_ref[pl.ds(v, SUBLANES, stride=groups), :]
        rsp_ref[...] = _inv_rms(ssqp_ref, d_model)
        for c0 in range(0, d, LANES):
            blk = h_ref[:, c0:c0 + LANES].astype(F32).reshape(SUBLANES, groups, LANES)
            hp_ref[:, c0:c0 + LANES] = jnp.swapaxes(blk, 0, 1).reshape(tm, LANES).astype(BF16)

    @pl.when((i == 0) & (j == 0))
    def _():
        tail_ref[...] = jnp.zeros_like(tail_ref)

    w = jnp.concatenate([_bf16(wg_ref[...]), _bf16(wv_ref[...])], axis=1)
    cw = jnp.concatenate([0.5 * cwg_ref[...], cwv_ref[...]], axis=1)
    cb = jnp.concatenate([0.5 * cbg_ref[...], cbv_ref[...]], axis=1)
    z_ref[halo:, :] = jnp.dot(hp_ref[...], w, preferred_element_type=F32) * rsp_ref[...]

    last_two = z_ref[tm:tm + halo, :]
    before = jnp.where(i > 0, tail_ref[j], 0.0)
    first = lax.broadcasted_iota(jnp.int32, (SUBLANES, 2 * tn), 0) < 1
    for g in range(2):
        cur = last_two[g * SUBLANES:(g + 1) * SUBLANES]
        prv = before[g * SUBLANES:(g + 1) * SUBLANES]
        z_ref[g * SUBLANES:(g + 1) * SUBLANES, :] = jnp.where(
            first, pltpu.roll(prv, 1, 0), pltpu.roll(cur, 1, 0))
    tail_ref[j] = last_two

    rows = min(FFN_EPILOGUE_ROWS, tm)
    for r0 in range(0, tm, rows):
        p0 = halo + r0
        conv = (cw[0:1] * z_ref[p0 - 2 * SUBLANES:p0 - 2 * SUBLANES + rows, :]
                + cw[1:2] * z_ref[p0 - SUBLANES:p0 - SUBLANES + rows, :]
                + cw[2:3] * z_ref[p0:p0 + rows, :] + cb)
        half_gate = conv[:, :tn]
        a_ref[r0:r0 + rows, :] = ((half_gate + half_gate * jnp.tanh(half_gate))
                                  * conv[:, tn:]).astype(BF16)


def _ffn_up(h, ssq, w_up_stack, layer, conv_w, conv_b):
    s, d = h.shape
    d_ff = w_up_stack.shape[2] // 2
    assert conv_w.shape[0] == 3
    tm = min(ROW_TILE, s)
    assert tm % (2 * SUBLANES * SUBLANES) == 0
    tn = _pick_tile(d_ff, 256)
    nj = d_ff // tn
    kern = functools.partial(_ffn_up_kernel, d_model=d)
    cb = conv_b.reshape(1, 2 * d_ff)
    return pl.pallas_call(
        kern,
        grid=(s // tm, nj),
        in_specs=[pl.BlockSpec((tm, d), lambda i, j: (i, 0), pipeline_mode=pl.Buffered(1)),
                  pl.BlockSpec((tm, LANES), lambda i, j: (i, 0)),
                  pl.BlockSpec((None, d, tn), lambda i, j: (layer, 0, j)),
                  pl.BlockSpec((None, d, tn), lambda i, j: (layer, 0, j + nj)),
                  pl.BlockSpec((3, tn), lambda i, j: (0, j)),
                  pl.BlockSpec((3, tn), lambda i, j: (0, j + nj)),
                  pl.BlockSpec((1, tn), lambda i, j: (0, j)),
                  pl.BlockSpec((1, tn), lambda i, j: (0, j + nj))],
        out_specs=pl.BlockSpec((tm, tn), lambda i, j: (i, j)),
        out_shape=jax.ShapeDtypeStruct((s, d_ff), BF16),
        scratch_shapes=[pltpu.VMEM((tm, d), BF16),
                        pltpu.VMEM((2 * SUBLANES + tm, 2 * tn), F32),
                        pltpu.VMEM((tm, LANES), F32),
                        pltpu.VMEM((tm, 1), F32),
                        pltpu.VMEM((nj, 2 * SUBLANES, 2 * tn), F32)],
        compiler_params=_params("arbitrary", "arbitrary"),
        name="ffn_up",
    )(h, ssq, w_up_stack, w_up_stack, conv_w, conv_w, cb, cb)


def _mm_heads_kernel(h_ref, ssq_ref, w_ref, o_ref, rs_ref, *, d_model):
    @pl.when(pl.program_id(1) == 0)
    def _():
        rs_ref[...] = _inv_rms(ssq_ref, d_model)

    acc = jnp.dot(h_ref[...], _bf16(w_ref[...]), preferred_element_type=F32) * rs_ref[...]
    for hh in range(o_ref.shape[0]):
        o_ref[hh] = acc[:, hh * HEAD_DIM:(hh + 1) * HEAD_DIM].astype(BF16)


def _mm_heads(h, ssq, w_stack, layer):
    s, d = h.shape
    n = w_stack.shape[2]
    tm = min(ROW_TILE, s)
    tn = _pick_tile(n, 512)
    hpt = tn // HEAD_DIM
    kern = functools.partial(_mm_heads_kernel, d_model=d)
    return pl.pallas_call(
        kern,
        grid=(s // tm, n // tn),
        in_specs=[pl.BlockSpec((tm, d), lambda i, j: (i, 0)),
                  pl.BlockSpec((tm, LANES), lambda i, j: (i, 0)),
                  pl.BlockSpec((None, d, tn), lambda i, j: (layer, 0, j))],
        out_specs=pl.BlockSpec((hpt, tm, HEAD_DIM), lambda i, j: (j, i, 0)),
        out_shape=jax.ShapeDtypeStruct((n // HEAD_DIM, s, HEAD_DIM), BF16),
        scratch_shapes=[pltpu.VMEM((tm, 1), F32)],
        compiler_params=_params("parallel", "arbitrary"),
        name="mm_heads",
    )(h, ssq, w_stack)


def _stick_kernel(q_ref, k_ref, v_ref, o_ref, decay_ref, acc_ref, *, scale, n_chains):
    tq = KEY_BLOCK
    bk = KEY_BLOCK
    n_groups = q_ref.shape[1] // (tq * n_chains)

    kj = lax.broadcasted_iota(jnp.int32, (2 * bk, 2 * bk), 0) % bk
    ks = lax.broadcasted_iota(jnp.int32, (2 * bk, 2 * bk), 1)
    suffix = jnp.where((kj > ks) | (ks >= bk), 1.0, 0.0).astype(BF16)
    r_io = lax.broadcasted_iota(jnp.int32, (tq, bk), 0)
    c_io = lax.broadcasted_iota(jnp.int32, (tq, bk), 1)
    causal = c_io < r_io

    def sweep(qi0, dist, mode):
        chains = range(n_chains)
        kbs = [qi0 + c - dist for c in chains]
        k0s = [pl.multiple_of(jnp.maximum(kb, 0) * bk, bk) for kb in kbs]

        def masked(c, x):
            if mode == "full":
                return x
            return jnp.where(causal if mode == "diagonal" else kbs[c] >= 0, x, 0.0)

        zs = []
        for c in chains:
            q = q_ref[0, pl.ds(pl.multiple_of((qi0 + c) * tq, tq), tq), :]
            k = k_ref[0, pl.ds(k0s[c], bk), :]
            zs.append(lax.dot_general(q, k, (((1,), (1,)), ((), ())),
                                      preferred_element_type=F32) * scale)
        sps = [masked(c, jnp.maximum(zs[c], 0.0) + jnp.log(1.0 + jnp.exp(-jnp.abs(zs[c]))))
               for c in chains]
        sums = []
        for c in chains:
            hi = sps[c].astype(BF16)
            lo = (sps[c] - hi.astype(F32)).astype(BF16)
            sums.append(jnp.dot(jnp.concatenate([hi, lo], axis=1), suffix,
                                preferred_element_type=F32))
        weights = [masked(c, jnp.exp(zs[c] - sps[c] - (decay_ref[c] + sums[c][:, :bk])))
                   for c in chains]
        decays = []
        for c in chains:
            v = v_ref[0, pl.ds(k0s[c], bk), :]
            acc_ref[c] += jnp.dot(weights[c].astype(BF16), v, preferred_element_type=F32)
            decays.append(decay_ref[c] + sums[c][:, bk:])
            decay_ref[c] = decays[c]
        return decays

    def any_live(decays, qi0, next_dist, all_have_blocks):
        m = None
        for c, dec in enumerate(decays):
            if not all_have_blocks:
                dec = jnp.where(qi0 + c - next_dist >= 0, dec, -2.0 * LOG_F32_TINY)
            m = dec if m is None else jnp.minimum(m, dec)
        return jnp.min(m) < -LOG_F32_TINY

    def group(g, carry):
        qi0 = g * n_chains
        decay_ref[...] = jnp.zeros_like(decay_ref)
        acc_ref[...] = jnp.zeros_like(acc_ref)
        decays = sweep(qi0, 0, "diagonal")

        def cond(state):
            return state[1]

        def body(state):
            dist = state[0]
            live = lax.cond(
                qi0 - dist - 1 >= 0,
                lambda: any_live(sweep(qi0, dist, "full"), qi0, dist + 1, True),
                lambda: any_live(sweep(qi0, dist, "partial"), qi0, dist + 1, False))
            return dist + 1, live

        lax.while_loop(cond, body, (jnp.int32(1), any_live(decays, qi0, 1, False)))
        for c in range(n_chains):
            o_ref[pl.ds(pl.multiple_of((qi0 + c) * tq, tq), tq), :] = acc_ref[c].astype(BF16)
        return carry

    lax.fori_loop(0, n_groups, group, 0)


def _stick_attention(qh, kvh):
    n_heads, s, _ = qh.shape
    n_chains = min(STICK_CHAINS, s // KEY_BLOCK)
    assert s % (KEY_BLOCK * n_chains) == 0
    kern = functools.partial(_stick_kernel, scale=HEAD_DIM ** -0.5, n_chains=n_chains)
    return pl.pallas_call(
        kern,
        grid=(n_heads,),
        in_specs=[pl.BlockSpec((1, s, HEAD_DIM), lambda h: (h, 0, 0)),
                  pl.BlockSpec((1, s, HEAD_DIM), lambda h: (h, 0, 0)),
                  pl.BlockSpec((1, s, HEAD_DIM), lambda h: (h + n_heads, 0, 0))],
        out_specs=pl.BlockSpec((s, HEAD_DIM), lambda h: (0, h)),
        out_shape=jax.ShapeDtypeStruct((s, n_heads * HEAD_DIM), BF16),
        scratch_shapes=[pltpu.VMEM((n_chains, KEY_BLOCK, KEY_BLOCK), F32),
                        pltpu.VMEM((n_chains, KEY_BLOCK, HEAD_DIM), F32)],
        compiler_params=_params("parallel"),
        name="stick_attention",
    )(qh, kvh, kvh)


def kernel(x, a_norm_g, a_w_in, a_ln_g, a_ln_b, a_w_s, a_b_s, a_w_out, b_norm_g, b_w_q, b_w_out,
           kv_norm_g, w_kv, ffn_norm_g, ffn_w_up, ffn_conv_w, ffn_conv_b, ffn_w_down, final_norm_g):
    bsz, s, d = x.shape
    assert bsz == 1
    n_a = a_w_in.shape[0]
    n_b = b_w_q.shape[0]
    depth = n_a + n_b
    xs = x[0]
    h, ssq = _prep(xs, a_norm_g[0])
    hkv = None
    kvh = None
    for layer in range(depth):
        if layer < n_a:
            i = layer
            z, vsum, vsq = _mm_in(h, ssq, a_w_in, i)
            y = _gate(z, vsum, vsq, a_ln_g[i], a_ln_b[i], a_w_s[i], a_b_s[i])
            xs, (h,), ssq = _mm_res(y, a_w_out, i, xs, [ffn_norm_g[layer]], 512)
        else:
            i = layer - n_a
            qh = _mm_heads(h, ssq, b_w_q, i)
            o = _stick_attention(qh, kvh)
            xs, (h,), ssq = _mm_res(o, b_w_out, i, xs, [ffn_norm_g[layer]], 512)
        a = _ffn_up(h, ssq, ffn_w_up, layer, ffn_conv_w[layer], ffn_conv_b[layer])
        if layer == n_a - 1:
            xs, (hkv, h), ssq = _mm_res(a, ffn_w_down, layer, xs, [kv_norm_g, b_norm_g[0]], 256, True)
            kvh = _mm_heads(hkv, ssq, w_kv[None], 0)
        elif layer == depth - 1:
            xs, _, ssq = _mm_res(a, ffn_w_down, layer, xs, [], 256, True)
        else:
            nxt = a_norm_g[layer + 1] if layer + 1 < n_a else b_norm_g[layer + 1 - n_a]
            xs, (h,), ssq = _mm_res(a, ffn_w_down, layer, xs, [nxt], 256, True)
    return _final_norm(xs, final_norm_g)[None]
```

```python
import functools

import jax
import jax.numpy as jnp
from jax import lax
from jax.experimental import pallas as pl
from jax.experimental.pallas import tpu as pltpu

EPS = 1e-6
CHUNK = 64
HEAD_DIM = 128
LANES = 128
SUBLANES = 8
MXU_COLS = 256
VMEM_LIMIT_BYTES = 56 * 1024 * 1024
ROW_TILE = 1024
FFN_EPILOGUE_ROWS = 256
KEY_BLOCK = 128
STICK_CHAINS = 16
LOG_F32_TINY = -104.0

F32 = jnp.float32
BF16 = jnp.bfloat16


def _pick_tile(n, target):
    for align in (MXU_COLS, LANES):
        best = None
        t = align
        while t <= min(n, target):
            if n % t == 0:
                best = t
            t += align
        if best is not None:
            return best
    raise ValueError(f"no lane-aligned tile for {n}")


def _params(*sem):
    return pltpu.CompilerParams(dimension_semantics=sem, vmem_limit_bytes=VMEM_LIMIT_BYTES)


def _bf16(w):
    return w if w.dtype == BF16 else w.astype(BF16)


def _fold_lanes(a):
    out = a[:, 0:LANES]
    for k in range(1, a.shape[1] // LANES):
        out = out + a[:, k * LANES:(k + 1) * LANES]
    return out


def _inv_rms(ssq_ref, d_model):
    return lax.rsqrt(jnp.sum(ssq_ref[...], axis=1, keepdims=True) / d_model + EPS)


def _prep_kernel(x_ref, g_ref, h_ref, ssq_ref):
    x = x_ref[...]
    h_ref[...] = (x * g_ref[...]).astype(BF16)
    ssq_ref[...] = _fold_lanes(x * x)


def _prep(x, g):
    s, d = x.shape
    tm = min(256, s)
    return pl.pallas_call(
        _prep_kernel,
        grid=(s // tm,),
        in_specs=[pl.BlockSpec((tm, d), lambda i: (i, 0)),
                  pl.BlockSpec((1, d), lambda i: (0, 0))],
        out_specs=[pl.BlockSpec((tm, d), lambda i: (i, 0)),
                   pl.BlockSpec((tm, LANES), lambda i: (i, 0))],
        out_shape=[jax.ShapeDtypeStruct((s, d), BF16),
                   jax.ShapeDtypeStruct((s, LANES), F32)],
        compiler_params=_params("parallel"),
        name="prep",
    )(x, g.reshape(1, d))


def _final_norm_kernel(x_ref, g_ref, o_ref):
    x = x_ref[...]
    inv = lax.rsqrt(jnp.mean(x * x, axis=1, keepdims=True) + EPS)
    o_ref[...] = x * inv * g_ref[...]


def _final_norm(x, g):
    s, d = x.shape
    tm = min(256, s)
    return pl.pallas_call(
        _final_norm_kernel,
        grid=(s // tm,),
        in_specs=[pl.BlockSpec((tm, d), lambda i: (i, 0)),
                  pl.BlockSpec((1, d), lambda i: (0, 0))],
        out_specs=pl.BlockSpec((tm, d), lambda i: (i, 0)),
        out_shape=jax.ShapeDtypeStruct((s, d), F32),
        compiler_params=_params("parallel"),
        name="final_norm",
    )(x, g.reshape(1, d))


def _mm_in_kernel(h_ref, ssq_ref, w_ref, z_ref, vsum_ref, vsq_ref, rs_ref, *, d_model, nj_half):
    j = pl.program_id(1)

    @pl.when(j == 0)
    def _():
        rs_ref[...] = _inv_rms(ssq_ref, d_model)

    acc = jnp.dot(h_ref[...], _bf16(w_ref[...]), preferred_element_type=F32)
    t = acc * rs_ref[...]
    z = 0.5 * t * (1.0 + lax.erf(t * (2.0 ** -0.5)))
    z_ref[...] = z.astype(BF16)

    @pl.when(j == nj_half)
    def _():
        vsum_ref[...] = jnp.zeros_like(vsum_ref)
        vsq_ref[...] = jnp.zeros_like(vsq_ref)

    @pl.when(j >= nj_half)
    def _():
        vsum_ref[...] += _fold_lanes(z)
        vsq_ref[...] += _fold_lanes(z * z)


def _mm_in(h, ssq, w_stack, layer):
    s, d = h.shape
    n = w_stack.shape[2]
    tm = min(ROW_TILE, s)
    tn = _pick_tile(n // 2, 512)
    nj = n // tn
    kern = functools.partial(_mm_in_kernel, d_model=d, nj_half=nj // 2)
    return pl.pallas_call(
        kern,
        grid=(s // tm, nj),
        in_specs=[pl.BlockSpec((tm, d), lambda i, j: (i, 0)),
                  pl.BlockSpec((tm, LANES), lambda i, j: (i, 0)),
                  pl.BlockSpec((None, d, tn), lambda i, j: (layer, 0, j))],
        out_specs=[pl.BlockSpec((tm, tn), lambda i, j: (i, j)),
                   pl.BlockSpec((tm, LANES), lambda i, j: (i, 0)),
                   pl.BlockSpec((tm, LANES), lambda i, j: (i, 0))],
        out_shape=[jax.ShapeDtypeStruct((s, n), BF16),
                   jax.ShapeDtypeStruct((s, LANES), F32),
                   jax.ShapeDtypeStruct((s, LANES), F32)],
        scratch_shapes=[pltpu.VMEM((tm, 1), F32)],
        compiler_params=_params("parallel", "arbitrary"),
        name="gmlp_in",
    )(h, ssq, w_stack)


def _gate_kernel(u_ref, v_ref, vsum_ref, vsq_ref, lng_ref, lnb_ref, ws_ref, bst_ref, y_ref,
                 *, width, n_groups, blk):
    mu = jnp.sum(vsum_ref[...], axis=1, keepdims=True) / width
    var = jnp.sum(vsq_ref[...], axis=1, keepdims=True) / width - mu * mu
    rstd = lax.rsqrt(var + EPS)
    vn = ((v_ref[...].astype(F32) - mu) * rstd * lng_ref[...] + lnb_ref[...]).astype(BF16)

    ri = lax.broadcasted_iota(jnp.int32, (blk, blk), 0)
    ci = lax.broadcasted_iota(jnp.int32, (blk, blk), 1)
    visible = (ci // CHUNK) <= (ri // CHUNK)
    gw = width // n_groups
    rows = u_ref.shape[0]
    for g in range(n_groups):
        wsg = jnp.where(visible, ws_ref[g], 0.0).astype(BF16)
        bias = bst_ref[:, g:g + 1]
        for n in range(rows // blk):
            r0 = n * blk
            sv = jnp.dot(wsg, vn[r0:r0 + blk, g * gw:(g + 1) * gw],
                         preferred_element_type=F32) + bias
            u = u_ref[r0:r0 + blk, g * gw:(g + 1) * gw].astype(F32)
            y_ref[r0:r0 + blk, g * gw:(g + 1) * gw] = (u * sv).astype(BF16)


def _gate(z, vsum, vsq, ln_g, ln_b, w_s, b_s):
    s = z.shape[0]
    width = z.shape[1] // 2
    n_groups, blk, _ = w_s.shape
    tb = min(512, s)
    kern = functools.partial(_gate_kernel, width=width, n_groups=n_groups, blk=blk)
    return pl.pallas_call(
        kern,
        grid=(s // tb,),
        in_specs=[pl.BlockSpec((tb, width), lambda i: (i, 0)),
                  pl.BlockSpec((tb, width), lambda i: (i, 1)),
                  pl.BlockSpec((tb, LANES), lambda i: (i, 0)),
                  pl.BlockSpec((tb, LANES), lambda i: (i, 0)),
                  pl.BlockSpec((1, width), lambda i: (0, 0)),
                  pl.BlockSpec((1, width), lambda i: (0, 0)),
                  pl.BlockSpec((n_groups, blk, blk), lambda i: (0, 0, 0)),
                  pl.BlockSpec((blk, n_groups), lambda i: (0, 0))],
        out_specs=pl.BlockSpec((tb, width), lambda i: (i, 0)),
        out_shape=jax.ShapeDtypeStruct((s, width), BF16),
        compiler_params=_params("parallel"),
        name="gmlp_gate",
    )(z, z, vsum, vsq, ln_g.reshape(1, width), ln_b.reshape(1, width), w_s, jnp.transpose(b_s))


def _mm_res_kernel(*refs, n_gains, rows_interleaved):
    a_ref, w_ref, xr_ref = refs[:3]
    g_refs = refs[3:3 + n_gains]
    x_out = refs[3 + n_gains]
    hn_outs = refs[4 + n_gains:4 + 2 * n_gains]
    ssq_out = refs[4 + 2 * n_gains]
    j = pl.program_id(1)

    acc = jnp.dot(a_ref[...], _bf16(w_ref[...]), preferred_element_type=F32)
    if rows_interleaved:
        perm_ref, nat_ref = refs[5 + 2 * n_gains:]
        tm = acc.shape[0]
        groups = tm // SUBLANES
        for c in range(perm_ref.shape[0]):
            perm_ref[c] = acc[:, c * LANES:(c + 1) * LANES]
            for u in range(groups):
                s_idx, v0 = divmod(u * SUBLANES, groups)
                nat_ref[u * SUBLANES:(u + 1) * SUBLANES, c * LANES:(c + 1) * LANES] = perm_ref[
                    c, pl.ds(v0 * SUBLANES + s_idx, SUBLANES, stride=SUBLANES), :]
        acc = nat_ref[...]
    xn = xr_ref[...] + acc
    x_out[...] = xn
    for g_ref, hn in zip(g_refs, hn_outs):
        hn[...] = (xn * g_ref[...]).astype(BF16)
    part = _fold_lanes(xn * xn)

    @pl.when(j == 0)
    def _():
        ssq_out[...] = part

    @pl.when(j > 0)
    def _():
        ssq_out[...] += part


def _mm_res(a, w_stack, layer, x_res, gains, tn_target, rows_interleaved=False):
    s, k = a.shape
    n = w_stack.shape[2]
    tm = min(ROW_TILE, s)
    tn = _pick_tile(n, tn_target)
    n_gains = len(gains)
    a_mode = {} if 2 * tm * k * 2 <= 16 * 1024 * 1024 else {"pipeline_mode": pl.Buffered(1)}
    kern = functools.partial(_mm_res_kernel, n_gains=n_gains, rows_interleaved=rows_interleaved)
    scratch = ([pltpu.VMEM((tn // LANES, tm, LANES), F32), pltpu.VMEM((tm, tn), F32)]
               if rows_interleaved else [])
    outs = pl.pallas_call(
        kern,
        grid=(s // tm, n // tn),
        in_specs=[pl.BlockSpec((tm, k), lambda i, j: (i, 0), **a_mode),
                  pl.BlockSpec((None, k, tn), lambda i, j: (layer, 0, j)),
                  pl.BlockSpec((tm, tn), lambda i, j: (i, j))]
                 + [pl.BlockSpec((1, tn), lambda i, j: (0, j))] * n_gains,
        out_specs=[pl.BlockSpec((tm, tn), lambda i, j: (i, j))] * (1 + n_gains)
                  + [pl.BlockSpec((tm, LANES), lambda i, j: (i, 0))],
        out_shape=[jax.ShapeDtypeStruct((s, n), F32)]
                  + [jax.ShapeDtypeStruct((s, n), BF16)] * n_gains
                  + [jax.ShapeDtypeStruct((s, LANES), F32)],
        scratch_shapes=scratch,
        compiler_params=_params("parallel", "arbitrary"),
        name="mm_res",
    )(a, w_stack, x_res, *[g.reshape(1, n) for g in gains])
    return outs[0], list(outs[1:1 + n_gains]), outs[1 + n_gains]


def _ffn_up_kernel(h_ref, ssq_ref, wg_ref, wv_ref, cwg_ref, cwv_ref, cbg_ref, cbv_ref,
                   a_ref, hp_ref, z_ref, ssqp_ref, rsp_ref, tail_ref, *, d_model):
    i = pl.program_id(0)
    j = pl.program_id(1)
    tm, tn = a_ref.shape
    d = h_ref.shape[1]
    groups = tm // SUBLANES
    halo = 2 * SUBLANES

    @pl.when(j == 0)
    def _():
        for v in range(groups):
            ssqp_ref[v * SUBLANES:(v + 1) * SUBLANES, :] = ssq_ref[pl.ds(v, SUBLANES, stride=groups), :]
        rsp_ref[...] = _inv_rms(ssqp_ref, d_model)
        for c0 in range(0, d, LANES):
            blk = h_ref[:, c0:c0 + LANES].astype(F32).reshape(SUBLANES, groups, LANES)
            hp_ref[:, c0:c0 + LANES] = jnp.swapaxes(blk, 0, 1).reshape(tm, LANES).astype(BF16)

    @pl.when((i == 0) & (j == 0))
    def _():
        tail_ref[...] = jnp.zeros_like(tail_ref)

    w = jnp.concatenate([_bf16(wg_ref[...]), _bf16(wv_ref[...])], axis=1)
    cw = jnp.concatenate([0.5 * cwg_ref[...], cwv_ref[...]], axis=1)
    cb = jnp.concatenate([0.5 * cbg_ref[...], cbv_ref[...]], axis=1)
    z_ref[halo:, :] = jnp.dot(hp_ref[...], w, preferred_element_type=F32) * rsp_ref[...]

    last_two = z_ref[tm:tm + halo, :]
    before = jnp.where(i > 0, tail_ref[j], 0.0)
    first = lax.broadcasted_iota(jnp.int32, (SUBLANES, 2 * tn), 0) < 1
    for g in range(2):
        cur = last_two[g * SUBLANES:(g + 1) * SUBLANES]
        prv = before[g * SUBLANES:(g + 1) * SUBLANES]
        z_ref[g * SUBLANES:(g + 1) * SUBLANES, :] = jnp.where(
            first, pltpu.roll(prv, 1, 0), pltpu.roll(cur, 1, 0))
    tail_ref[j] = last_two

    rows = min(FFN_EPILOGUE_ROWS, tm)
    for r0 in range(0, tm, rows):
        p0 = halo + r0
        conv = (cw[0:1] * z_ref[p0 - 2 * SUBLANES:p0 - 2 * SUBLANES + rows, :]
                + cw[1:2] * z_ref[p0 - SUBLANES:p0 - SUBLANES + rows, :]
                + cw[2:3] * z_ref[p0:p0 + rows, :] + cb)
        half_gate = conv[:, :tn]
        a_ref[r0:r0 + rows, :] = ((half_gate + half_gate * jnp.tanh(half_gate))
                                  * conv[:, tn:]).astype(BF16)


def _ffn_up(h, ssq, w_up_stack, layer, conv_w, conv_b):
    s, d = h.shape
    d_ff = w_up_stack.shape[2] // 2
    assert conv_w.shape[0] == 3
    tm = min(ROW_TILE, s)
    assert tm % (2 * SUBLANES * SUBLANES) == 0
    tn = _pick_tile(d_ff, 256)
    nj = d_ff // tn
    kern = functools.partial(_ffn_up_kernel, d_model=d)
    cb = conv_b.reshape(1, 2 * d_ff)
    return pl.pallas_call(
        kern,
        grid=(s // tm, nj),
        in_specs=[pl.BlockSpec((tm, d), lambda i, j: (i, 0), pipeline_mode=pl.Buffered(1)),
                  pl.BlockSpec((tm, LANES), lambda i, j: (i, 0)),
                  pl.BlockSpec((None, d, tn), lambda i, j: (layer, 0, j)),
                  pl.BlockSpec((None, d, tn), lambda i, j: (layer, 0, j + nj)),
                  pl.BlockSpec((3, tn), lambda i, j: (0, j)),
                  pl.BlockSpec((3, tn), lambda i, j: (0, j + nj)),
                  pl.BlockSpec((1, tn), lambda i, j: (0, j)),
                  pl.BlockSpec((1, tn), lambda i, j: (0, j + nj))],
        out_specs=pl.BlockSpec((tm, tn), lambda i, j: (i, j)),
        out_shape=jax.ShapeDtypeStruct((s, d_ff), BF16),
        scratch_shapes=[pltpu.VMEM((tm, d), BF16),
                        pltpu.VMEM((2 * SUBLANES + tm, 2 * tn), F32),
                        pltpu.VMEM((tm, LANES), F32),
                        pltpu.VMEM((tm, 1), F32),
                        pltpu.VMEM((nj, 2 * SUBLANES, 2 * tn), F32)],
        compiler_params=_params("arbitrary", "arbitrary"),
        name="ffn_up",
    )(h, ssq, w_up_stack, w_up_stack, conv_w, conv_w, cb, cb)


def _mm_heads_kernel(h_ref, ssq_ref, w_ref, o_ref, rs_ref, *, d_model):
    @pl.when(pl.program_id(1) == 0)
    def _():
        rs_ref[...] = _inv_rms(ssq_ref, d_model)

    acc = jnp.dot(h_ref[...], _bf16(w_ref[...]), preferred_element_type=F32) * rs_ref[...]
    for hh in range(o_ref.shape[0]):
        o_ref[hh] = acc[:, hh * HEAD_DIM:(hh + 1) * HEAD_DIM].astype(BF16)


def _mm_heads(h, ssq, w_stack, layer):
    s, d = h.shape
    n = w_stack.shape[2]
    tm = min(ROW_TILE, s)
    tn = _pick_tile(n, 512)
    hpt = tn // HEAD_DIM
    kern = functools.partial(_mm_heads_kernel, d_model=d)
    return pl.pallas_call(
        kern,
        grid=(s // tm, n // tn),
        in_specs=[pl.BlockSpec((tm, d), lambda i, j: (i, 0)),
                  pl.BlockSpec((tm, LANES), lambda i, j: (i, 0)),
                  pl.BlockSpec((None, d, tn), lambda i, j: (layer, 0, j))],
        out_specs=pl.BlockSpec((hpt, tm, HEAD_DIM), lambda i, j: (j, i, 0)),
        out_shape=jax.ShapeDtypeStruct((n // HEAD_DIM, s, HEAD_DIM), BF16),
        scratch_shapes=[pltpu.VMEM((tm, 1), F32)],
        compiler_params=_params("parallel", "arbitrary"),
        name="mm_heads",
    )(h, ssq, w_stack)


def _stick_kernel(q_ref, k_ref, v_ref, o_ref, decay_ref, acc_ref, *, scale, n_chains):
    tq = KEY_BLOCK
    bk = KEY_BLOCK
    n_groups = q_ref.shape[1] // (tq * n_chains)

    kj = lax.broadcasted_iota(jnp.int32, (2 * bk, 2 * bk), 0) % bk
    ks = lax.broadcasted_iota(jnp.int32, (2 * bk, 2 * bk), 1)
    suffix = jnp.where((kj > ks) | (ks >= bk), 1.0, 0.0).astype(BF16)
    r_io = lax.broadcasted_iota(jnp.int32, (tq, bk), 0)
    c_io = lax.broadcasted_iota(jnp.int32, (tq, bk), 1)
    causal = c_io < r_io

    def sweep(qi0, dist, mode):
        chains = range(n_chains)
        kbs = [qi0 + c - dist for c in chains]
        k0s = [pl.multiple_of(jnp.maximum(kb, 0) * bk, bk) for kb in kbs]

        def masked(c, x):
            if mode == "full":
                return x
            return jnp.where(causal if mode == "diagonal" else kbs[c] >= 0, x, 0.0)

        zs = []
        for c in chains:
            q = q_ref[0, pl.ds(pl.multiple_of((qi0 + c) * tq, tq), tq), :]
            k = k_ref[0, pl.ds(k0s[c], bk), :]
            zs.append(lax.dot_general(q, k, (((1,), (1,)), ((), ())),
                                      preferred_element_type=F32) * scale)
        sps = [masked(c, jnp.maximum(zs[c], 0.0) + jnp.log(1.0 + jnp.exp(-jnp.abs(zs[c]))))
               for c in chains]
        sums = []
        for c in chains:
            hi = sps[c].astype(BF16)
            lo = (sps[c] - hi.astype(F32)).astype(BF16)
            sums.append(jnp.dot(jnp.concatenate([hi, lo], axis=1), suffix,
                                preferred_element_type=F32))
        weights = [masked(c, jnp.exp(zs[c] - sps[c] - (decay_ref[c] + sums[c][:, :bk])))
                   for c in chains]
        decays = []
        for c in chains:
            v = v_ref[0, pl.ds(k0s[c], bk), :]
            acc_ref[c] += jnp.dot(weights[c].astype(BF16), v, preferred_element_type=F32)
            decays.append(decay_ref[c] + sums[c][:, bk:])
            decay_ref[c] = decays[c]
        return decays

    def any_live(decays, qi0, next_dist, all_have_blocks):
        m = None
        for c, dec in enumerate(decays):
            if not all_have_blocks:
                dec = jnp.where(qi0 + c - next_dist >= 0, dec, -2.0 * LOG_F32_TINY)
            m = dec if m is None else jnp.minimum(m, dec)
        return jnp.min(m) < -LOG_F32_TINY

    def group(g, carry):
        qi0 = g * n_chains
        decay_ref[...] = jnp.zeros_like(decay_ref)
        acc_ref[...] = jnp.zeros_like(acc_ref)
        decays = sweep(qi0, 0, "diagonal")

        def cond(state):
            return state[1]

        def body(state):
            dist = state[0]
            live = lax.cond(
                qi0 - dist - 1 >= 0,
                lambda: any_live(sweep(qi0, dist, "full"), qi0, dist + 1, True),
                lambda: any_live(sweep(qi0, dist, "partial"), qi0, dist + 1, False))
            return dist + 1, live

        lax.while_loop(cond, body, (jnp.int32(1), any_live(decays, qi0, 1, False)))
        for c in range(n_chains):
            o_ref[pl.ds(pl.multiple_of((qi0 + c) * tq, tq), tq), :] = acc_ref[c].astype(BF16)
        return carry

    lax.fori_loop(0, n_groups, group, 0)


def _stick_attention(qh, kvh):
    n_heads, s, _ = qh.shape
    n_chains = min(STICK_CHAINS, s // KEY_BLOCK)
    assert s % (KEY_BLOCK * n_chains) == 0
    kern = functools.partial(_stick_kernel, scale=HEAD_DIM ** -0.5, n_chains=n_chains)
    return pl.pallas_call(
        kern,
        grid=(n_heads,),
        in_specs=[pl.BlockSpec((1, s, HEAD_DIM), lambda h: (h, 0, 0)),
                  pl.BlockSpec((1, s, HEAD_DIM), lambda h: (h, 0, 0)),
                  pl.BlockSpec((1, s, HEAD_DIM), lambda h: (h + n_heads, 0, 0))],
        out_specs=pl.BlockSpec((s, HEAD_DIM), lambda h: (0, h)),
        out_shape=jax.ShapeDtypeStruct((s, n_heads * HEAD_DIM), BF16),
        scratch_shapes=[pltpu.VMEM((n_chains, KEY_BLOCK, KEY_BLOCK), F32),
                        pltpu.VMEM((n_chains, KEY_BLOCK, HEAD_DIM), F32)],
        compiler_params=_params("parallel"),
        name="stick_attention",
    )(qh, kvh, kvh)


def kernel(x, a_norm_g, a_w_in, a_ln_g, a_ln_b, a_w_s, a_b_s, a_w_out, b_norm_g, b_w_q, b_w_out,
           kv_norm_g, w_kv, ffn_norm_g, ffn_w_up, ffn_conv_w, ffn_conv_b, ffn_w_down, final_norm_g):
    bsz, s, d = x.shape
    assert bsz == 1
    n_a = a_w_in.shape[0]
    n_b = b_w_q.shape[0]
    depth = n_a + n_b
    xs = x[0]
    ffn_w_up = ffn_w_up.astype(BF16)
    h, ssq = _prep(xs, a_norm_g[0])
    hkv = None
    kvh = None
    for layer in range(depth):
        if layer < n_a:
            i = layer
            z, vsum, vsq = _mm_in(h, ssq, a_w_in, i)
            y = _gate(z, vsum, vsq, a_ln_g[i], a_ln_b[i], a_w_s[i], a_b_s[i])
            xs, (h,), ssq = _mm_res(y, a_w_out, i, xs, [ffn_norm_g[layer]], 512)
        else:
            i = layer - n_a
            qh = _mm_heads(h, ssq, b_w_q, i)
            o = _stick_attention(qh, kvh)
            xs, (h,), ssq = _mm_res(o, b_w_out, i, xs, [ffn_norm_g[layer]], 512)
        a = _ffn_up(h, ssq, ffn_w_up, layer, ffn_conv_w[layer], ffn_conv_b[layer])
        if layer == n_a - 1:
            xs, (hkv, h), ssq = _mm_res(a, ffn_w_down, layer, xs, [kv_norm_g, b_norm_g[0]], 256, True)
            kvh = _mm_heads(hkv, ssq, w_kv[None], 0)
        elif layer == depth - 1:
            xs, _, ssq = _mm_res(a, ffn_w_down, layer, xs, [], 256, True)
        else:
            nxt = a_norm_g[layer + 1] if layer + 1 < n_a else b_norm_g[layer + 1 - n_a]
            xs, (h,), ssq = _mm_res(a, ffn_w_down, layer, xs, [nxt], 256, True)
    return _final_norm(xs, final_norm_g)[None]
```

```python
import functools

import jax
import jax.numpy as jnp
from jax import lax
from jax.experimental import pallas as pl
from jax.experimental.pallas import tpu as pltpu

EPS = 1e-6
CHUNK = 64
HEAD_DIM = 128
LANES = 128
SUBLANES = 8
MXU_COLS = 256
VMEM_LIMIT_BYTES = 56 * 1024 * 1024
ROW_TILE = 1024
FFN_EPILOGUE_ROWS = 256
KEY_BLOCK = 128
STICK_CHAINS = 16
LOG_F32_TINY = -104.0

F32 = jnp.float32
BF16 = jnp.bfloat16


def _pick_tile(n, target):
    for align in (MXU_COLS, LANES):
        best = None
        t = align
        while t <= min(n, target):
            if n % t == 0:
                best = t
            t += align
        if best is not None:
            return best
    raise ValueError(f"no lane-aligned tile for {n}")


def _params(*sem):
    return pltpu.CompilerParams(dimension_semantics=sem, vmem_limit_bytes=VMEM_LIMIT_BYTES)


def _bf16(w):
    return w if w.dtype == BF16 else w.astype(BF16)


def _fold_lanes(a):
    out = a[:, 0:LANES]
    for k in range(1, a.shape[1] // LANES):
        out = out + a[:, k * LANES:(k + 1) * LANES]
    return out


def _inv_rms(ssq_ref, d_model):
    return lax.rsqrt(jnp.sum(ssq_ref[...], axis=1, keepdims=True) / d_model + EPS)


def _prep_kernel(x_ref, g_ref, h_ref, ssq_ref):
    x = x_ref[...]
    h_ref[...] = (x * g_ref[...]).astype(BF16)
    ssq_ref[...] = _fold_lanes(x * x)


def _prep(x, g):
    s, d = x.shape
    tm = min(256, s)
    return pl.pallas_call(
        _prep_kernel,
        grid=(s // tm,),
        in_specs=[pl.BlockSpec((tm, d), lambda i: (i, 0)),
                  pl.BlockSpec((1, d), lambda i: (0, 0))],
        out_specs=[pl.BlockSpec((tm, d), lambda i: (i, 0)),
                   pl.BlockSpec((tm, LANES), lambda i: (i, 0))],
        out_shape=[jax.ShapeDtypeStruct((s, d), BF16),
                   jax.ShapeDtypeStruct((s, LANES), F32)],
        compiler_params=_params("parallel"),
        name="prep",
    )(x, g.reshape(1, d))


def _final_norm_kernel(x_ref, g_ref, o_ref):
    x = x_ref[...]
    inv = lax.rsqrt(jnp.mean(x * x, axis=1, keepdims=True) + EPS)
    o_ref[...] = x * inv * g_ref[...]


def _final_norm(x, g):
    s, d = x.shape
    tm = min(256, s)
    return pl.pallas_call(
        _final_norm_kernel,
        grid=(s // tm,),
        in_specs=[pl.BlockSpec((tm, d), lambda i: (i, 0)),
                  pl.BlockSpec((1, d), lambda i: (0, 0))],
        out_specs=pl.BlockSpec((tm, d), lambda i: (i, 0)),
        out_shape=jax.ShapeDtypeStruct((s, d), F32),
        compiler_params=_params("parallel"),
        name="final_norm",
    )(x, g.reshape(1, d))


def _mm_in_kernel(h_ref, ssq_ref, w_ref, z_ref, vsum_ref, vsq_ref, rs_ref, *, d_model, nj_half):
    j = pl.program_id(1)

    @pl.when(j == 0)
    def _():
        rs_ref[...] = _inv_rms(ssq_ref, d_model)

    acc = jnp.dot(h_ref[...], _bf16(w_ref[...]), preferred_element_type=F32)
    t = acc * rs_ref[...]
    z = 0.5 * t * (1.0 + lax.erf(t * (2.0 ** -0.5)))
    z_ref[...] = z.astype(BF16)

    @pl.when(j == nj_half)
    def _():
        vsum_ref[...] = jnp.zeros_like(vsum_ref)
        vsq_ref[...] = jnp.zeros_like(vsq_ref)

    @pl.when(j >= nj_half)
    def _():
        vsum_ref[...] += _fold_lanes(z)
        vsq_ref[...] += _fold_lanes(z * z)


def _mm_in(h, ssq, w_stack, layer):
    s, d = h.shape
    n = w_stack.shape[2]
    tm = min(ROW_TILE, s)
    tn = _pick_tile(n // 2, 512)
    nj = n // tn
    kern = functools.partial(_mm_in_kernel, d_model=d, nj_half=nj // 2)
    return pl.pallas_call(
        kern,
        grid=(s // tm, nj),
        in_specs=[pl.BlockSpec((tm, d), lambda i, j: (i, 0)),
                  pl.BlockSpec((tm, LANES), lambda i, j: (i, 0)),
                  pl.BlockSpec((None, d, tn), lambda i, j: (layer, 0, j))],
        out_specs=[pl.BlockSpec((tm, tn), lambda i, j: (i, j)),
                   pl.BlockSpec((tm, LANES), lambda i, j: (i, 0)),
                   pl.BlockSpec((tm, LANES), lambda i, j: (i, 0))],
        out_shape=[jax.ShapeDtypeStruct((s, n), BF16),
                   jax.ShapeDtypeStruct((s, LANES), F32),
                   jax.ShapeDtypeStruct((s, LANES), F32)],
        scratch_shapes=[pltpu.VMEM((tm, 1), F32)],
        compiler_params=_params("parallel", "arbitrary"),
        name="gmlp_in",
    )(h, ssq, w_stack)


def _gate_kernel(u_ref, v_ref, vsum_ref, vsq_ref, lng_ref, lnb_ref, ws_ref, bst_ref, y_ref,
                 *, width, n_groups, blk):
    mu = jnp.sum(vsum_ref[...], axis=1, keepdims=True) / width
    var = jnp.sum(vsq_ref[...], axis=1, keepdims=True) / width - mu * mu
    rstd = lax.rsqrt(var + EPS)
    vn = ((v_ref[...].astype(F32) - mu) * rstd * lng_ref[...] + lnb_ref[...]).astype(BF16)

    ri = lax.broadcasted_iota(jnp.int32, (blk, blk), 0)
    ci = lax.broadcasted_iota(jnp.int32, (blk, blk), 1)
    visible = (ci // CHUNK) <= (ri // CHUNK)
    gw = width // n_groups
    rows = u_ref.shape[0]
    for g in range(n_groups):
        wsg = jnp.where(visible, ws_ref[g], 0.0).astype(BF16)
        bias = bst_ref[:, g:g + 1]
        for n in range(rows // blk):
            r0 = n * blk
            sv = jnp.dot(wsg, vn[r0:r0 + blk, g * gw:(g + 1) * gw],
                         preferred_element_type=F32) + bias
            u = u_ref[r0:r0 + blk, g * gw:(g + 1) * gw].astype(F32)
            y_ref[r0:r0 + blk, g * gw:(g + 1) * gw] = (u * sv).astype(BF16)


def _gate(z, vsum, vsq, ln_g, ln_b, w_s, b_s):
    s = z.shape[0]
    width = z.shape[1] // 2
    n_groups, blk, _ = w_s.shape
    tb = min(512, s)
    kern = functools.partial(_gate_kernel, width=width, n_groups=n_groups, blk=blk)
    return pl.pallas_call(
        kern,
        grid=(s // tb,),
        in_specs=[pl.BlockSpec((tb, width), lambda i: (i, 0)),
                  pl.BlockSpec((tb, width), lambda i: (i, 1)),
                  pl.BlockSpec((tb, LANES), lambda i: (i, 0)),
                  pl.BlockSpec((tb, LANES), lambda i: (i, 0)),
                  pl.BlockSpec((1, width), lambda i: (0, 0)),
                  pl.BlockSpec((1, width), lambda i: (0, 0)),
                  pl.BlockSpec((n_groups, blk, blk), lambda i: (0, 0, 0)),
                  pl.BlockSpec((blk, n_groups), lambda i: (0, 0))],
        out_specs=pl.BlockSpec((tb, width), lambda i: (i, 0)),
        out_shape=jax.ShapeDtypeStruct((s, width), BF16),
        compiler_params=_params("parallel"),
        name="gmlp_gate",
    )(z, z, vsum, vsq, ln_g.reshape(1, width), ln_b.reshape(1, width), w_s, jnp.transpose(b_s))


def _mm_res_kernel(*refs, n_gains, rows_interleaved):
    a_ref, w_ref, xr_ref = refs[:3]
    g_refs = refs[3:3 + n_gains]
    x_out = refs[3 + n_gains]
    hn_outs = refs[4 + n_gains:4 + 2 * n_gains]
    ssq_out = refs[4 + 2 * n_gains]
    j = pl.program_id(1)

    acc = jnp.dot(a_ref[...], _bf16(w_ref[...]), preferred_element_type=F32)
    if rows_interleaved:
        perm_ref, nat_ref = refs[5 + 2 * n_gains:]
        tm = acc.shape[0]
        groups = tm // SUBLANES
        for c in range(perm_ref.shape[0]):
            perm_ref[c] = acc[:, c * LANES:(c + 1) * LANES]
            for u in range(groups):
                s_idx, v0 = divmod(u * SUBLANES, groups)
                nat_ref[u * SUBLANES:(u + 1) * SUBLANES, c * LANES:(c + 1) * LANES] = perm_ref[
                    c, pl.ds(v0 * SUBLANES + s_idx, SUBLANES, stride=SUBLANES), :]
        acc = nat_ref[...]
    xn = xr_ref[...] + acc
    x_out[...] = xn
    for g_ref, hn in zip(g_refs, hn_outs):
        hn[...] = (xn * g_ref[...]).astype(BF16)
    part = _fold_lanes(xn * xn)

    @pl.when(j == 0)
    def _():
        ssq_out[...] = part

    @pl.when(j > 0)
    def _():
        ssq_out[...] += part


def _mm_res(a, w_stack, layer, x_res, gains, tn_target, rows_interleaved=False):
    s, k = a.shape
    n = w_stack.shape[2]
    tm = min(ROW_TILE, s)
    tn = _pick_tile(n, tn_target)
    n_gains = len(gains)
    a_mode = {} if 2 * tm * k * 2 <= 16 * 1024 * 1024 else {"pipeline_mode": pl.Buffered(1)}
    kern = functools.partial(_mm_res_kernel, n_gains=n_gains, rows_interleaved=rows_interleaved)
    scratch = ([pltpu.VMEM((tn // LANES, tm, LANES), F32), pltpu.VMEM((tm, tn), F32)]
               if rows_interleaved else [])
    outs = pl.pallas_call(
        kern,
        grid=(s // tm, n // tn),
        in_specs=[pl.BlockSpec((tm, k), lambda i, j: (i, 0), **a_mode),
                  pl.BlockSpec((None, k, tn), lambda i, j: (layer, 0, j)),
                  pl.BlockSpec((tm, tn), lambda i, j: (i, j))]
                 + [pl.BlockSpec((1, tn), lambda i, j: (0, j))] * n_gains,
        out_specs=[pl.BlockSpec((tm, tn), lambda i, j: (i, j))] * (1 + n_gains)
                  + [pl.BlockSpec((tm, LANES), lambda i, j: (i, 0))],
        out_shape=[jax.ShapeDtypeStruct((s, n), F32)]
                  + [jax.ShapeDtypeStruct((s, n), BF16)] * n_gains
                  + [jax.ShapeDtypeStruct((s, LANES), F32)],
        scratch_shapes=scratch,
        compiler_params=_params("parallel", "arbitrary"),
        name="mm_res",
    )(a, w_stack, x_res, *[g.reshape(1, n) for g in gains])
    return outs[0], list(outs[1:1 + n_gains]), outs[1 + n_gains]


def _ffn_up_kernel(h_ref, ssq_ref, wg_ref, wv_ref, cwg_ref, cwv_ref, cbg_ref, cbv_ref,
                   a_ref, hp_ref, stage_ref, z_ref, ssqp_ref, rsp_ref, tail_ref, *, d_model):
    i = pl.program_id(0)
    j = pl.program_id(1)
    tm, tn = a_ref.shape
    d = h_ref.shape[1]
    groups = tm // SUBLANES
    halo = 2 * SUBLANES

    @pl.when(j == 0)
    def _():
        for v in range(groups):
            ssqp_ref[v * SUBLANES:(v + 1) * SUBLANES, :] = ssq_ref[pl.ds(v, SUBLANES, stride=groups), :]
        rsp_ref[...] = _inv_rms(ssqp_ref, d_model)
        for c0 in range(0, d, LANES):
            stage_ref[...] = h_ref[:, c0:c0 + LANES].astype(F32)
            for w in range(groups // 2):
                pair = jnp.concatenate(
                    [stage_ref[pl.ds(2 * w, SUBLANES, stride=groups), :],
                     stage_ref[pl.ds(2 * w + 1, SUBLANES, stride=groups), :]], axis=0)
                hp_ref[2 * SUBLANES * w:2 * SUBLANES * (w + 1), c0:c0 + LANES] = pair.astype(BF16)

    @pl.when((i == 0) & (j == 0))
    def _():
        tail_ref[...] = jnp.zeros_like(tail_ref)

    w = jnp.concatenate([_bf16(wg_ref[...]), _bf16(wv_ref[...])], axis=1)
    cw = jnp.concatenate([0.5 * cwg_ref[...], cwv_ref[...]], axis=1)
    cb = jnp.concatenate([0.5 * cbg_ref[...], cbv_ref[...]], axis=1)
    z_ref[halo:, :] = jnp.dot(hp_ref[...], w, preferred_element_type=F32) * rsp_ref[...]

    last_two = z_ref[tm:tm + halo, :]
    before = jnp.where(i > 0, tail_ref[j], 0.0)
    first = lax.broadcasted_iota(jnp.int32, (SUBLANES, 2 * tn), 0) < 1
    for g in range(2):
        cur = last_two[g * SUBLANES:(g + 1) * SUBLANES]
        prv = before[g * SUBLANES:(g + 1) * SUBLANES]
        z_ref[g * SUBLANES:(g + 1) * SUBLANES, :] = jnp.where(
            first, pltpu.roll(prv, 1, 0), pltpu.roll(cur, 1, 0))
    tail_ref[j] = last_two

    rows = min(FFN_EPILOGUE_ROWS, tm)
    for r0 in range(0, tm, rows):
        p0 = halo + r0
        conv = (cw[0:1] * z_ref[p0 - 2 * SUBLANES:p0 - 2 * SUBLANES + rows, :]
                + cw[1:2] * z_ref[p0 - SUBLANES:p0 - SUBLANES + rows, :]
                + cw[2:3] * z_ref[p0:p0 + rows, :] + cb)
        half_gate = conv[:, :tn]
        a_ref[r0:r0 + rows, :] = ((half_gate + half_gate * jnp.tanh(half_gate))
                                  * conv[:, tn:]).astype(BF16)


def _ffn_up(h, ssq, w_up_stack, layer, conv_w, conv_b):
    s, d = h.shape
    d_ff = w_up_stack.shape[2] // 2
    assert conv_w.shape[0] == 3
    tm = min(ROW_TILE, s)
    assert tm % (2 * SUBLANES * SUBLANES) == 0
    tn = _pick_tile(d_ff, 256)
    nj = d_ff // tn
    kern = functools.partial(_ffn_up_kernel, d_model=d)
    cb = conv_b.reshape(1, 2 * d_ff)
    return pl.pallas_call(
        kern,
        grid=(s // tm, nj),
        in_specs=[pl.BlockSpec((tm, d), lambda i, j: (i, 0), pipeline_mode=pl.Buffered(1)),
                  pl.BlockSpec((tm, LANES), lambda i, j: (i, 0)),
                  pl.BlockSpec((None, d, tn), lambda i, j: (layer, 0, j)),
                  pl.BlockSpec((None, d, tn), lambda i, j: (layer, 0, j + nj)),
                  pl.BlockSpec((3, tn), lambda i, j: (0, j)),
                  pl.BlockSpec((3, tn), lambda i, j: (0, j + nj)),
                  pl.BlockSpec((1, tn), lambda i, j: (0, j)),
                  pl.BlockSpec((1, tn), lambda i, j: (0, j + nj))],
        out_specs=pl.BlockSpec((tm, tn), lambda i, j: (i, j)),
        out_shape=jax.ShapeDtypeStruct((s, d_ff), BF16),
        scratch_shapes=[pltpu.VMEM((tm, d), BF16),
                        pltpu.VMEM((tm, LANES), F32),
                        pltpu.VMEM((2 * SUBLANES + tm, 2 * tn), F32),
                        pltpu.VMEM((tm, LANES), F32),
                        pltpu.VMEM((tm, 1), F32),
                        pltpu.VMEM((nj, 2 * SUBLANES, 2 * tn), F32)],
        compiler_params=_params("arbitrary", "arbitrary"),
        name="ffn_up",
    )(h, ssq, w_up_stack, w_up_stack, conv_w, conv_w, cb, cb)


def _mm_heads_kernel(h_ref, ssq_ref, w_ref, o_ref, rs_ref, *, d_model):
    @pl.when(pl.program_id(1) == 0)
    def _():
        rs_ref[...] = _inv_rms(ssq_ref, d_model)

    acc = jnp.dot(h_ref[...], _bf16(w_ref[...]), preferred_element_type=F32) * rs_ref[...]
    for hh in range(o_ref.shape[0]):
        o_ref[hh] = acc[:, hh * HEAD_DIM:(hh + 1) * HEAD_DIM].astype(BF16)


def _mm_heads(h, ssq, w_stack, layer):
    s, d = h.shape
    n = w_stack.shape[2]
    tm = min(ROW_TILE, s)
    tn = _pick_tile(n, 512)
    hpt = tn // HEAD_DIM
    kern = functools.partial(_mm_heads_kernel, d_model=d)
    return pl.pallas_call(
        kern,
        grid=(s // tm, n // tn),
        in_specs=[pl.BlockSpec((tm, d), lambda i, j: (i, 0)),
                  pl.BlockSpec((tm, LANES), lambda i, j: (i, 0)),
                  pl.BlockSpec((None, d, tn), lambda i, j: (layer, 0, j))],
        out_specs=pl.BlockSpec((hpt, tm, HEAD_DIM), lambda i, j: (j, i, 0)),
        out_shape=jax.ShapeDtypeStruct((n // HEAD_DIM, s, HEAD_DIM), BF16),
        scratch_shapes=[pltpu.VMEM((tm, 1), F32)],
        compiler_params=_params("parallel", "arbitrary"),
        name="mm_heads",
    )(h, ssq, w_stack)


def _stick_kernel(q_ref, k_ref, v_ref, o_ref, decay_ref, acc_ref, *, scale, n_chains):
    tq = KEY_BLOCK
    bk = KEY_BLOCK
    n_groups = q_ref.shape[1] // (tq * n_chains)

    kj = lax.broadcasted_iota(jnp.int32, (2 * bk, 2 * bk), 0) % bk
    ks = lax.broadcasted_iota(jnp.int32, (2 * bk, 2 * bk), 1)
    suffix = jnp.where((kj > ks) | (ks >= bk), 1.0, 0.0).astype(BF16)
    r_io = lax.broadcasted_iota(jnp.int32, (tq, bk), 0)
    c_io = lax.broadcasted_iota(jnp.int32, (tq, bk), 1)
    causal = c_io < r_io

    def sweep(qi0, dist, mode):
        chains = range(n_chains)
        kbs = [qi0 + c - dist for c in chains]
        k0s = [pl.multiple_of(jnp.maximum(kb, 0) * bk, bk) for kb in kbs]

        def masked(c, x):
            if mode == "full":
                return x
            return jnp.where(causal if mode == "diagonal" else kbs[c] >= 0, x, 0.0)

        zs = []
        for c in chains:
            q = q_ref[0, pl.ds(pl.multiple_of((qi0 + c) * tq, tq), tq), :]
            k = k_ref[0, pl.ds(k0s[c], bk), :]
            zs.append(lax.dot_general(q, k, (((1,), (1,)), ((), ())),
                                      preferred_element_type=F32) * scale)
        sps = [masked(c, jnp.maximum(zs[c], 0.0) + jnp.log(1.0 + jnp.exp(-jnp.abs(zs[c]))))
               for c in chains]
        sums = []
        for c in chains:
            hi = sps[c].astype(BF16)
            lo = (sps[c] - hi.astype(F32)).astype(BF16)
            sums.append(jnp.dot(jnp.concatenate([hi, lo], axis=1), suffix,
                                preferred_element_type=F32))
        weights = [masked(c, jnp.exp(zs[c] - sps[c] - (decay_ref[c] + sums[c][:, :bk])))
                   for c in chains]
        decays = []
        for c in chains:
            v = v_ref[0, pl.ds(k0s[c], bk), :]
            acc_ref[c] += jnp.dot(weights[c].astype(BF16), v, preferred_element_type=F32)
            decays.append(decay_ref[c] + sums[c][:, bk:])
            decay_ref[c] = decays[c]
        return decays

    def any_live(decays, qi0, next_dist, all_have_blocks):
        m = None
        for c, dec in enumerate(decays):
            if not all_have_blocks:
                dec = jnp.where(qi0 + c - next_dist >= 0, dec, -2.0 * LOG_F32_TINY)
            m = dec if m is None else jnp.minimum(m, dec)
        return jnp.min(m) < -LOG_F32_TINY

    def group(g, carry):
        qi0 = g * n_chains
        decay_ref[...] = jnp.zeros_like(decay_ref)
        acc_ref[...] = jnp.zeros_like(acc_ref)
        decays = sweep(qi0, 0, "diagonal")

        def cond(state):
            return state[1]

        def body(state):
            dist = state[0]
            live = lax.cond(
                qi0 - dist - 1 >= 0,
                lambda: any_live(sweep(qi0, dist, "full"), qi0, dist + 1, True),
                lambda: any_live(sweep(qi0, dist, "partial"), qi0, dist + 1, False))
            return dist + 1, live

        lax.while_loop(cond, body, (jnp.int32(1), any_live(decays, qi0, 1, False)))
        for c in range(n_chains):
            o_ref[pl.ds(pl.multiple_of((qi0 + c) * tq, tq), tq), :] = acc_ref[c].astype(BF16)
        return carry

    lax.fori_loop(0, n_groups, group, 0)


def _stick_attention(qh, kvh):
    n_heads, s, _ = qh.shape
    n_chains = min(STICK_CHAINS, s // KEY_BLOCK)
    assert s % (KEY_BLOCK * n_chains) == 0
    kern = functools.partial(_stick_kernel, scale=HEAD_DIM ** -0.5, n_chains=n_chains)
    return pl.pallas_call(
        kern,
        grid=(n_heads,),
        in_specs=[pl.BlockSpec((1, s, HEAD_DIM), lambda h: (h, 0, 0)),
                  pl.BlockSpec((1, s, HEAD_DIM), lambda h: (h, 0, 0)),
                  pl.BlockSpec((1, s, HEAD_DIM), lambda h: (h + n_heads, 0, 0))],
        out_specs=pl.BlockSpec((s, HEAD_DIM), lambda h: (0, h)),
        out_shape=jax.ShapeDtypeStruct((s, n_heads * HEAD_DIM), BF16),
        scratch_shapes=[pltpu.VMEM((n_chains, KEY_BLOCK, KEY_BLOCK), F32),
                        pltpu.VMEM((n_chains, KEY_BLOCK, HEAD_DIM), F32)],
        compiler_params=_params("parallel"),
        name="stick_attention",
    )(qh, kvh, kvh)


def kernel(x, a_norm_g, a_w_in, a_ln_g, a_ln_b, a_w_s, a_b_s, a_w_out, b_norm_g, b_w_q, b_w_out,
           kv_norm_g, w_kv, ffn_norm_g, ffn_w_up, ffn_conv_w, ffn_conv_b, ffn_w_down, final_norm_g):
    bsz, s, d = x.shape
    assert bsz == 1
    n_a = a_w_in.shape[0]
    n_b = b_w_q.shape[0]
    depth = n_a + n_b
    xs = x[0]
    h, ssq = _prep(xs, a_norm_g[0])
    hkv = None
    kvh = None
    for layer in range(depth):
        if layer < n_a:
            i = layer
            z, vsum, vsq = _mm_in(h, ssq, a_w_in, i)
            y = _gate(z, vsum, vsq, a_ln_g[i], a_ln_b[i], a_w_s[i], a_b_s[i])
            xs, (h,), ssq = _mm_res(y, a_w_out, i, xs, [ffn_norm_g[layer]], 512)
        else:
            i = layer - n_a
            qh = _mm_heads(h, ssq, b_w_q, i)
            o = _stick_attention(qh, kvh)
            xs, (h,), ssq = _mm_res(o, b_w_out, i, xs, [ffn_norm_g[layer]], 512)
        a = _ffn_up(h, ssq, ffn_w_up, layer, ffn_conv_w[layer], ffn_conv_b[layer])
        if layer == n_a - 1:
            xs, (hkv, h), ssq = _mm_res(a, ffn_w_down, layer, xs, [kv_norm_g, b_norm_g[0]], 256, True)
            kvh = _mm_heads(hkv, ssq, w_kv[None], 0)
        elif layer == depth - 1:
            xs, _, ssq = _mm_res(a, ffn_w_down, layer, xs, [], 256, True)
        else:
            nxt = a_norm_g[layer + 1] if layer + 1 < n_a else b_norm_g[layer + 1 - n_a]
            xs, (h,), ssq = _mm_res(a, ffn_w_down, layer, xs, [nxt], 256, True)
    return _final_norm(xs, final_norm_g)[None]
```
